```python
import jax, jax.numpy as jnp
from jax import lax
import numpy as np


D_MODEL = 1024
BATCH = 4
SEQ = 8192
DEPTH = 2

HEAD_DIM = 64
CONV_CH = D_MODEL // 4
CONV_WIDTH = 31
ATT_WIDTH = 3 * D_MODEL // 8
N_ATT_HEADS = ATT_WIDTH // HEAD_DIM
DILATION_PAIRS = ((128, 1), (512, 4), (2048, 16))
ATT_BLOCK = 128
ALIBI_MAX_EXP = 8.0
MASK_VALUE = -1e30
REC_WIDTH = 3 * D_MODEL // 8
REC_KEY_DIM = 64
REC_VAL_DIM = 64
N_REC_HEADS = REC_WIDTH // REC_VAL_DIM
REC_KEY_WIDTH = N_REC_HEADS * REC_KEY_DIM
REC_CHUNK = 64
F_TINY = 1e-30
MIX_WIDTH = CONV_CH + ATT_WIDTH + REC_WIDTH
IN_SPLITS = (CONV_CH, CONV_CH, ATT_WIDTH, ATT_WIDTH, ATT_WIDTH,
             REC_KEY_WIDTH, REC_KEY_WIDTH, REC_KEY_WIDTH, REC_WIDTH, REC_WIDTH)
IN_COLS = sum(IN_SPLITS)
D_FF = ((8 * D_MODEL // 3 + 127) // 128) * 128
FFN_CONV_WIDTH = 3
N_MOD = 6
EPS = 1e-6

kernel_name = "hybrid_conv_dilattn_hgrn2_encoder"


def _rmsnorm(x, w):
    xf = x.astype(jnp.float32)
    y = xf * lax.rsqrt(jnp.mean(xf * xf, axis=-1, keepdims=True) + EPS)
    return (y * w.astype(jnp.float32)).astype(x.dtype)


def _layernorm(x, w, b):
    xf = x.astype(jnp.float32)
    mu = jnp.mean(xf, axis=-1, keepdims=True)
    var = jnp.mean(jnp.square(xf - mu), axis=-1, keepdims=True)
    y = (xf - mu) * lax.rsqrt(var + EPS)
    return (y * w.astype(jnp.float32) + b.astype(jnp.float32)).astype(x.dtype)


def _depthwise_conv(x, w, b=None):
    k_w, ch = w.shape
    y = lax.conv_general_dilated(
        x, w[:, None, :].astype(x.dtype), window_strides=(1,),
        padding=[(k_w // 2, k_w // 2)],
        dimension_numbers=('NWC', 'WIO', 'NWC'), feature_group_count=ch)
    if b is not None:
        y = y + b.astype(x.dtype)
    return y


def _dilated_window_attention(q, k, v, slopes, window, dilation):
    B, S, H, Dh = q.shape
    half = window // (2 * dilation)
    L = S // dilation

    def to_sub(t):
        return t.reshape(B, L, dilation, H, Dh).transpose(0, 2, 1, 3, 4).reshape(B * dilation, L, H, Dh)

    qs, ks, vs = to_sub(q), to_sub(k), to_sub(v)
    blk = min(ATT_BLOCK, L)
    nb = -(-L // blk)
    Lp = nb * blk
    span = blk + 2 * half
    qs = jnp.pad(qs, ((0, 0), (0, Lp - L), (0, 0), (0, 0)))
    pad_k = ((0, 0), (half, half + Lp - L), (0, 0), (0, 0))
    ks = jnp.pad(ks, pad_k)
    vs = jnp.pad(vs, pad_k)
    idx = (np.arange(nb) * blk)[:, None] + np.arange(span)[None, :]
    kb = ks[:, idx]
    vb = vs[:, idx]
    qb = qs.reshape(B * dilation, nb, blk, H, Dh)
    rel = np.arange(span)[None, :] - half - np.arange(blk)[:, None]
    key_pos = idx - half
    valid = ((np.abs(rel) <= half)[None]
             & (key_pos[:, None, :] >= 0) & (key_pos[:, None, :] < L))
    dist = jnp.asarray(np.abs(rel) * dilation, jnp.float32)
    scores = jnp.einsum('bnqhd,bnkhd->bnhqk', qb, kb) * (Dh ** -0.5)
    scores = scores - slopes[:, None, None] * dist
    scores = jnp.where(jnp.asarray(valid)[:, None], scores, MASK_VALUE)
    lse = jax.nn.logsumexp(scores, axis=-1)
    p = jnp.exp(scores - lse[..., None])
    o = jnp.einsum('bnhqk,bnkhd->bnqhd', p, vb).reshape(B * dilation, Lp, H, Dh)[:, :L]
    lse = lse.transpose(0, 1, 3, 2).reshape(B * dilation, Lp, H)[:, :L]

    def from_sub(t):
        t5 = t.reshape((B, dilation, L) + t.shape[2:])
        t5 = jnp.moveaxis(t5, 1, 2)
        return t5.reshape((B, S) + t5.shape[3:])

    return from_sub(o), from_sub(lse)


def _mixture_of_dilated_attention(q, k, v):
    slopes = jnp.asarray(2.0 ** (-ALIBI_MAX_EXP * np.arange(1, N_ATT_HEADS + 1) / N_ATT_HEADS), jnp.float32)
    outs, lses = [], []
    for window, dilation in DILATION_PAIRS:
        o_g, l_g = _dilated_window_attention(q, k, v, slopes, window, dilation)
        outs.append(o_g)
        lses.append(l_g)
    wts = jax.nn.softmax(jnp.stack(lses, 0), axis=0)
    return jnp.einsum('gbsh,gbshd->bshd', wts, jnp.stack(outs, 0))


def _hgrn2_scan(q, k, v, logf):
    B, S, H, dk = q.shape
    dv = v.shape[-1]
    n = S // REC_CHUNK

    def chunks(t):
        return t.reshape(B, n, REC_CHUNK, H, t.shape[-1]).transpose(1, 0, 3, 2, 4)

    qc, kc, vc = chunks(q), chunks(k), chunks(v)
    bc = jnp.cumsum(chunks(logf), axis=3)
    lower = jnp.asarray(np.tril(np.ones((REC_CHUNK, REC_CHUNK), bool)))[:, :, None]

    def step(state, inp):
        qt, kt, vt, bt = inp
        diff = bt[:, :, :, None, :] - bt[:, :, None, :, :]
        decay = jnp.where(lower, jnp.exp(jnp.where(lower, diff, 0.0)), 0.0)
        scores = jnp.einsum('bhtk,bhsk,bhtsk->bhts', qt, kt, decay)
        o = (jnp.einsum('bhts,bhsv->bhtv', scores, vt)
             + jnp.einsum('bhtk,bhkv->bhtv', qt * jnp.exp(bt), state))
        b_last = bt[:, :, -1:, :]
        state = (jnp.exp(b_last[:, :, 0, :])[..., None] * state
                 + jnp.einsum('bhsk,bhsv->bhkv', kt * jnp.exp(b_last - bt), vt))
        return state, o

    state0 = jnp.zeros((B, H, dk, dv), q.dtype)
    _, o = lax.scan(step, state0, (qc, kc, vc, bc))
    return o.transpose(1, 0, 3, 2, 4).reshape(B, S, H, dv)


def _hgrn2_gate(z, lb):
    f = lb + (1.0 - lb) * jax.nn.sigmoid(z)
    logf = jnp.log(jnp.maximum(f, F_TINY))
    k = (1.0 - lb) * jax.nn.sigmoid(-z)
    return logf, k


def _token_mixers(h, w_in, conv_w, conv_b, ln_w, ln_b, lb_fwd, lb_bwd, rec_norm_w, w_out):
    B, S, _ = h.shape
    f32 = jnp.float32
    proj = h @ w_in.astype(h.dtype)
    cuts = [int(c) for c in np.cumsum(IN_SPLITS)[:-1]]
    (a_val, a_gate, q_att, k_att, v_att,
     q_rec, z_fwd, z_bwd, i_rec, g_rec) = jnp.split(proj, cuts, axis=-1)

    a = a_val * jax.nn.sigmoid(a_gate)
    a = _depthwise_conv(a, conv_w, conv_b)
    a = jax.nn.silu(_layernorm(a, ln_w, ln_b))

    def heads(t, d):
        return t.astype(f32).reshape(B, S, -1, d)
    att = _mixture_of_dilated_attention(heads(q_att, HEAD_DIM), heads(k_att, HEAD_DIM), heads(v_att, HEAD_DIM))
    att = att.reshape(B, S, ATT_WIDTH).astype(h.dtype)

    qr = heads(jax.nn.silu(q_rec.astype(f32)), REC_KEY_DIM)
    vr = heads(i_rec, REC_VAL_DIM)
    logf_f, k_f = _hgrn2_gate(z_fwd.astype(f32), lb_fwd)
    logf_b, k_b = _hgrn2_gate(z_bwd.astype(f32), lb_bwd)
    o_f = _hgrn2_scan(qr, heads(k_f, REC_KEY_DIM), vr, heads(logf_f, REC_KEY_DIM))
    flip = lambda t: jnp.flip(t, axis=1)
    o_b = flip(_hgrn2_scan(flip(qr), flip(heads(k_b, REC_KEY_DIM)), flip(vr), flip(heads(logf_b, REC_KEY_DIM))))
    o_r = o_f + o_b
    o_r = o_r * lax.rsqrt(jnp.mean(o_r * o_r, axis=-1, keepdims=True) + EPS)
    o_r = o_r.reshape(B, S, REC_WIDTH) * rec_norm_w.astype(f32)
    rec = (o_r * jax.nn.silu(g_rec.astype(f32))).astype(h.dtype)

    mixed = jnp.concatenate([a, att, rec], axis=-1)
    return mixed @ w_out.astype(h.dtype)


def _conv_ffn(h, w_up, conv_w, w_down):
    u = h @ w_up.astype(h.dtype)
    u = _depthwise_conv(u, conv_w)
    gate, val = jnp.split(u, 2, axis=-1)
    return (jax.nn.gelu(gate, approximate=False) * val) @ w_down.astype(h.dtype)


def setup_inputs(seed: int = 0) -> dict:
    key = jax.random.key(seed)
    ks = jax.random.split(key, 18)
    f32 = jnp.float32
    D = D_MODEL

    def nrm(k, shape, scale):
        return jax.random.normal(k, shape, f32) * scale

    return {
        "x": nrm(ks[0], (BATCH, SEQ, D), 1.0),
        "c": nrm(ks[1], (BATCH, D), 1.0),
        "w_ada": nrm(ks[2], (DEPTH, D, N_MOD * D), 0.5 * D ** -0.5),
        "b_ada": nrm(ks[3], (DEPTH, N_MOD * D), 0.02),
        "norm1_w": 1.0 + nrm(ks[4], (DEPTH, D), 0.05),
        "w_in": nrm(ks[5], (DEPTH, D, IN_COLS), D ** -0.5),
        "conv_a_w": nrm(ks[6], (DEPTH, CONV_WIDTH, CONV_CH), CONV_WIDTH ** -0.5),
        "conv_a_b": nrm(ks[7], (DEPTH, CONV_CH), 0.02),
        "ln_a_w": 1.0 + nrm(ks[8], (DEPTH, CONV_CH), 0.05),
        "ln_a_b": nrm(ks[9], (DEPTH, CONV_CH), 0.02),
        "lb_gamma": nrm(ks[10], (DEPTH, 2, REC_KEY_WIDTH), 1.0),
        "rec_norm_w": 1.0 + nrm(ks[11], (DEPTH, REC_WIDTH), 0.05),
        "w_out": nrm(ks[12], (DEPTH, MIX_WIDTH, D), MIX_WIDTH ** -0.5),
        "norm2_w": 1.0 + nrm(ks[13], (DEPTH, D), 0.05),
        "w_up": nrm(ks[14], (DEPTH, D, 2 * D_FF), D ** -0.5),
        "conv_f_w": nrm(ks[15], (DEPTH, FFN_CONV_WIDTH, 2 * D_FF), FFN_CONV_WIDTH ** -0.5),
        "w_down": nrm(ks[16], (DEPTH, D_FF, D), D_FF ** -0.5),
        "final_norm_w": 1.0 + nrm(ks[17], (D,), 0.05),
    }


def reference(x, c, w_ada, b_ada, norm1_w, w_in, conv_a_w, conv_a_b, ln_a_w, ln_a_b,
              lb_gamma, rec_norm_w, w_out, norm2_w, w_up, conv_f_w, w_down, final_norm_w):
    p = jax.nn.softmax(lb_gamma.astype(jnp.float32), axis=0)
    lower_bounds = jnp.cumsum(p, axis=0) - p[0:1]
    cond = jax.nn.silu(c)
    for l in range(DEPTH):
        mod = (cond @ w_ada[l] + b_ada[l]).astype(x.dtype)[:, None, :]
        sh1, sc1, g1, sh2, sc2, g2 = jnp.split(mod, N_MOD, axis=-1)
        h = _rmsnorm(x, norm1_w[l]) * (1.0 + sc1) + sh1
        x = x + g1 * _token_mixers(h, w_in[l], conv_a_w[l], conv_a_b[l], ln_a_w[l], ln_a_b[l],
                                   lower_bounds[l, 0], lower_bounds[l, 1], rec_norm_w[l], w_out[l])
        h = _rmsnorm(x, norm2_w[l]) * (1.0 + sc2) + sh2
        x = x + g2 * _conv_ffn(h, w_up[l], conv_f_w[l], w_down[l])
    return _rmsnorm(x, final_norm_w)
```

```python
import functools

import numpy as np
import jax
import jax.numpy as jnp
from jax import lax
from jax.experimental import pallas as pl
from jax.experimental.pallas import tpu as pltpu

F32 = jnp.float32
BF16 = jnp.bfloat16

D_MODEL = 1024
HEAD_DIM = 64
CONV_CH = 256
CONV_WIDTH = 31
ATT_WIDTH = 384
N_ATT_HEADS = 6
DILATIONS = (1, 4, 16)
BAND_HALF = 64
ATT_BLOCK = 128
ALIBI_MAX_EXP = 8.0
MASK_VALUE = -1e30
REC_WIDTH = 384
F_TINY = 1e-30
D_FF = 2816
N_MOD = 6
EPS = 1e-6
IN_COLS = 3584

LANES = 128
PAIR = 2 * HEAD_DIM
N_PAIRS = REC_WIDTH // PAIR
REC_CHUNK = 16
VMEM_LIMIT = 56 * 1024 * 1024


def _params(sem, vmem=VMEM_LIMIT):
    return pltpu.CompilerParams(dimension_semantics=sem, vmem_limit_bytes=vmem)


def _const_spec(shape):
    nd = len(shape)
    return pl.BlockSpec(shape, lambda *_: (0,) * nd, pipeline_mode=pl.Buffered(1))


def _sigmoids(z):
    e = jnp.exp(-jnp.abs(z))
    r = 1.0 / (1.0 + e)
    er = e * r
    pos = z >= 0
    return jnp.where(pos, r, er), jnp.where(pos, er, r)


def _silu(z):
    return z * _sigmoids(z)[0]


def _rms(x, w):
    ms = jnp.mean(x * x, axis=-1, keepdims=True)
    return x * lax.rsqrt(ms + EPS) * w


def _mod_kernel(c_ref, w_ref, b_ref, o_ref):
    cond = _silu(c_ref[...])
    o_ref[0] = jnp.dot(cond, w_ref[0], preferred_element_type=F32) + b_ref[0]


def _modulation(c, w_ada, b_ada):
    depth, d, n = w_ada.shape
    bsz = c.shape[0]
    bp = -(-bsz // 8) * 8
    cp = jnp.zeros((bp, d), F32).at[:bsz].set(c)
    tn = 1536
    out = pl.pallas_call(
        _mod_kernel,
        grid=(depth, n // tn),
        in_specs=[pl.BlockSpec((bp, d), lambda l, j: (0, 0)),
                  pl.BlockSpec((1, d, tn), lambda l, j: (l, 0, j)),
                  pl.BlockSpec((1, 1, tn), lambda l, j: (l, 0, j))],
        out_specs=pl.BlockSpec((1, bp, tn), lambda l, j: (l, 0, j)),
        out_shape=jax.ShapeDtypeStruct((depth, bp, n), F32),
        compiler_params=_params(("arbitrary", "arbitrary")),
        name="adaln_mod",
    )(cp, w_ada, b_ada.reshape(depth, 1, n))
    return out[:, :bsz].reshape(depth, bsz, N_MOD, d)


def _inproj_kernel(x_ref, mod_ref, nw_ref, w_ref, lbg_ref,
                   a_ref, q_ref, k_ref, v_ref,
                   qr_ref, kf_ref, lf_ref, kb_ref, lb_ref, vr_ref, gr_ref, *, layer):
    mod = mod_ref[0]
    h = (_rms(x_ref[...], nw_ref[...]) * (1.0 + mod[1:2, :]) + mod[0:1, :]).astype(BF16)

    def proj(c0, n):
        return jnp.dot(h, w_ref[:, c0:c0 + n], preferred_element_type=F32)

    ag = proj(0, 2 * CONV_CH)
    a_ref[...] = ag[:, :CONV_CH] * _sigmoids(ag[:, CONV_CH:])[0]
    c0 = 2 * CONV_CH
    q_ref[...] = (proj(c0, ATT_WIDTH) * (HEAD_DIM ** -0.5)).astype(BF16)
    k_ref[...] = proj(c0 + ATT_WIDTH, ATT_WIDTH).astype(BF16)
    v_ref[...] = proj(c0 + 2 * ATT_WIDTH, ATT_WIDTH).astype(BF16)
    c0 += 3 * ATT_WIDTH

    g = lbg_ref[...]
    ge = jnp.exp(g - jnp.max(g, axis=0, keepdims=True))
    p = ge / jnp.sum(ge, axis=0, keepdims=True)
    lbv = jnp.zeros(p.shape[1:], F32)
    for j in range(1, layer + 1):
        lbv = lbv + p[j]

    zq = proj(c0, REC_WIDTH)
    zf = proj(c0 + REC_WIDTH, REC_WIDTH)
    zb = proj(c0 + 2 * REC_WIDTH, REC_WIDTH)
    zi = proj(c0 + 3 * REC_WIDTH, REC_WIDTH)
    zg = proj(c0 + 4 * REC_WIDTH, REC_WIDTH)

    def gate(z, lb):
        sp, sn = _sigmoids(z)
        f = lb + (1.0 - lb) * sp
        return jnp.log(jnp.maximum(f, F_TINY)), (1.0 - lb) * sn

    qs = _silu(zq)
    gs = _silu(zg)
    lff, kff = gate(zf, lbv[0:1, :])
    lbb, kbb = gate(zb, lbv[1:2, :])
    for pr in range(N_PAIRS):
        sl = slice(pr * PAIR, (pr + 1) * PAIR)
        qr_ref[pr] = qs[:, sl]
        gr_ref[pr] = gs[:, sl]
        vr_ref[pr] = zi[:, sl]
        lf_ref[pr] = lff[:, sl]
        kf_ref[pr] = kff[:, sl]
        lb_ref[pr] = lbb[:, sl]
        kb_ref[pr] = kbb[:, sl]


def _inproj(x2, mod_l, norm_w, w_in_bf, lb_gamma, layer, seq, tm):
    n, d = x2.shape
    tiles_per_seq = seq // tm
    row = lambda i: (i, 0)
    rec_spec = pl.BlockSpec((N_PAIRS, tm, PAIR), lambda i: (0, i, 0))
    rec_shape = jax.ShapeDtypeStruct((N_PAIRS, n, PAIR), F32)
    return pl.pallas_call(
        functools.partial(_inproj_kernel, layer=layer),
        grid=(n // tm,),
        in_specs=[pl.BlockSpec((tm, d), row),
                  pl.BlockSpec((1, N_MOD, d), lambda i: (i // tiles_per_seq, 0, 0)),
                  _const_spec((1, d)),
                  _const_spec((d, IN_COLS)),
                  _const_spec(lb_gamma.shape)],
        out_specs=[pl.BlockSpec((tm, CONV_CH), row)] + [pl.BlockSpec((tm, ATT_WIDTH), row)] * 3
                  + [rec_spec] * 7,
        out_shape=[jax.ShapeDtypeStruct((n, CONV_CH), F32)]
                  + [jax.ShapeDtypeStruct((n, ATT_WIDTH), BF16)] * 3 + [rec_shape] * 7,
        compiler_params=_params(("arbitrary",)),
        name="inproj",
    )(x2, mod_l, norm_w.reshape(1, d), w_in_bf, lb_gamma)


CONV_HALO = 16
CONV_ROWS = 64


def _conva_kernel(a_ref, ap_ref, an_ref, w_ref, b_ref, lw_ref, lb_ref, o_ref, ext_ref, *, seq, tm):
    i = pl.program_id(0)
    t0 = (i * tm) % seq
    ext_ref[0:CONV_HALO] = jnp.where(t0 == 0, 0.0, ap_ref[...])
    ext_ref[CONV_HALO:CONV_HALO + tm] = a_ref[...]
    ext_ref[CONV_HALO + tm:] = jnp.where(t0 + tm == seq, 0.0, an_ref[...])
    w = w_ref[...]
    shift = CONV_HALO - CONV_WIDTH // 2
    for r0 in range(0, tm, CONV_ROWS):
        acc = jnp.zeros((CONV_ROWS, CONV_CH), F32) + b_ref[...]
        for j in range(CONV_WIDTH):
            acc = acc + w[j:j + 1, :] * ext_ref[r0 + j + shift:r0 + j + shift + CONV_ROWS, :]
        mu = jnp.mean(acc, axis=-1, keepdims=True)
        cen = acc - mu
        var = jnp.mean(cen * cen, axis=-1, keepdims=True)
        y = cen * lax.rsqrt(var + EPS) * lw_ref[...] + lb_ref[...]
        o_ref[r0:r0 + CONV_ROWS, :] = _silu(y).astype(BF16)


def _conv_a(a, conv_w, conv_b, ln_w, ln_b, seq, tm):
    n, ch = a.shape
    hb = tm // CONV_HALO
    nhb = n // CONV_HALO
    vec = lambda v: v.reshape(1, ch)
    return pl.pallas_call(
        functools.partial(_conva_kernel, seq=seq, tm=tm),
        grid=(n // tm,),
        in_specs=[pl.BlockSpec((tm, ch), lambda i: (i, 0)),
                  pl.BlockSpec((CONV_HALO, ch), lambda i: (jnp.maximum(i * hb - 1, 0), 0)),
                  pl.BlockSpec((CONV_HALO, ch), lambda i: (jnp.minimum((i + 1) * hb, nhb - 1), 0)),
                  _const_spec((CONV_WIDTH, ch)), _const_spec((1, ch)),
                  _const_spec((1, ch)), _const_spec((1, ch))],
        out_specs=pl.BlockSpec((tm, ch), lambda i: (i, 0)),
        out_shape=jax.ShapeDtypeStruct((n, ch), BF16),
        scratch_shapes=[pltpu.VMEM((tm + 2 * CONV_HALO, ch), F32)],
        compiler_params=_params(("arbitrary",)),
        name="conv_a",
    )(a, a, a, conv_w, vec(conv_b), vec(ln_w), vec(ln_b))


def _attn_kernel(q_ref, k_ref, kp_ref, kn_ref, v_ref, vp_ref, vn_ref, bias_ref,
                 o_ref, l_ref, kx_ref, vx_ref, *, tq, sub_len):
    i = pl.program_id(1)
    kx_ref[0:BAND_HALF] = kp_ref[0]
    kx_ref[BAND_HALF:BAND_HALF + tq] = k_ref[0]
    kx_ref[BAND_HALF + tq:] = kn_ref[0]
    vx_ref[0:BAND_HALF] = vp_ref[0]
    vx_ref[BAND_HALF:BAND_HALF + tq] = v_ref[0]
    vx_ref[BAND_HALF + tq:] = vn_ref[0]
    span = ATT_BLOCK + 2 * BAND_HALF
    first = lax.broadcasted_iota(jnp.int32, (ATT_BLOCK, PAIR), 1) < HEAD_DIM
    col = lax.broadcasted_iota(jnp.int32, (1, span), 1)
    nt = (((1,), (1,)), ((), ()))
    for blk in range(tq // ATT_BLOCK):
        r0 = blk * ATT_BLOCK
        kpos = i * tq + (r0 - BAND_HALF) + col
        valid = (kpos >= 0) & (kpos < sub_len)
        for pr in range(N_PAIRS):
            sl = slice(pr * PAIR, (pr + 1) * PAIR)
            qp = q_ref[0, r0:r0 + ATT_BLOCK, sl]
            kp = kx_ref[r0:r0 + span, sl]
            vp = vx_ref[r0:r0 + span, sl]
            res = []
            for hh in range(2):
                keep = first if hh == 0 else jnp.logical_not(first)
                qm = jnp.where(keep, qp, jnp.zeros_like(qp))
                s = lax.dot_general(qm, kp, nt, preferred_element_type=F32)
                s = jnp.where(valid, s + bias_ref[2 * pr + hh], MASK_VALUE)
                m = jnp.max(s, axis=-1, keepdims=True)
                pe = jnp.exp(s - m)
                den = jnp.sum(pe, axis=-1, keepdims=True)
                o = jnp.dot(pe.astype(BF16), vp, preferred_element_type=F32)
                res.append((o / den, m + jnp.log(den)))
            o_ref[0, r0:r0 + ATT_BLOCK, sl] = jnp.where(first, res[0][0], res[1][0])
            l_ref[0, r0:r0 + ATT_BLOCK, sl] = jnp.where(first, res[0][1], res[1][1])


def _attn_bias(dilation):
    slopes = 2.0 ** (-ALIBI_MAX_EXP * np.arange(1, N_ATT_HEADS + 1) / N_ATT_HEADS)
    span = ATT_BLOCK + 2 * BAND_HALF
    rel = np.arange(span)[None, :] - BAND_HALF - np.arange(ATT_BLOCK)[:, None]
    dist = (np.abs(rel) * dilation).astype(np.float32)
    bias = -(slopes.astype(np.float32)[:, None, None] * dist[None])
    bias = np.where((np.abs(rel) <= BAND_HALF)[None], bias, np.float32(MASK_VALUE))
    return jnp.asarray(bias, F32)


def _attn_branch(q, k, v, bsz, seq, dilation):
    n = q.shape[0]
    sub_len = seq // dilation
    bd = bsz * dilation

    def to_sub(t):
        if dilation == 1:
            return t.reshape(bd, sub_len, ATT_WIDTH)
        return t.reshape(bsz, sub_len, dilation, ATT_WIDTH).transpose(0, 2, 1, 3).reshape(bd, sub_len, ATT_WIDTH)

    def from_sub(t):
        if dilation == 1:
            return t.reshape(n, ATT_WIDTH)
        return t.reshape(bsz, dilation, sub_len, ATT_WIDTH).transpose(0, 2, 1, 3).reshape(n, ATT_WIDTH)

    tq = min(sub_len, 1024)
    hb = tq // BAND_HALF
    nhb = sub_len // BAND_HALF
    main = pl.BlockSpec((1, tq, ATT_WIDTH), lambda b, i: (b, i, 0))
    prev = pl.BlockSpec((1, BAND_HALF, ATT_WIDTH), lambda b, i: (b, jnp.maximum(i * hb - 1, 0), 0))
    nxt = pl.BlockSpec((1, BAND_HALF, ATT_WIDTH), lambda b, i: (b, jnp.minimum((i + 1) * hb, nhb - 1), 0))
    qs, ks, vs = to_sub(q), to_sub(k), to_sub(v)
    span = ATT_BLOCK + 2 * BAND_HALF
    o, lse = pl.pallas_call(
        functools.partial(_attn_kernel, tq=tq, sub_len=sub_len),
        grid=(bd, sub_len // tq),
        in_specs=[main, main, prev, nxt, main, prev, nxt,
                  _const_spec((N_ATT_HEADS, ATT_BLOCK, span))],
        out_specs=[main, main],
        out_shape=[jax.ShapeDtypeStruct((bd, sub_len, ATT_WIDTH), F32)] * 2,
        scratch_shapes=[pltpu.VMEM((tq + 2 * BAND_HALF, ATT_WIDTH), BF16)] * 2,
        compiler_params=_params(("arbitrary", "arbitrary")),
        name=f"attn_d{dilation}",
    )(qs, ks, ks, ks, vs, vs, vs, _attn_bias(dilation))
    return from_sub(o), from_sub(lse)


def _cumsum_rows(tri, x):
    hi = x.astype(BF16)
    r1 = x - hi.astype(F32)
    mid = r1.astype(BF16)
    lo = (r1 - mid.astype(F32)).astype(BF16)
    dot = lambda t: jnp.dot(tri, t, preferred_element_type=F32)
    return dot(hi) + dot(mid) + dot(lo)


def _rec_chunk(q_ref, k_ref, v_ref, b_ref, st_ref, o_ref, ones_ref, bd_ref, pr, r, reverse):
    c = REC_CHUNK
    rows = pl.ds(r, c)
    q = q_ref[pr, rows, :]
    k = k_ref[pr, rows, :]
    v = v_ref[pr, rows, :]
    b = b_ref[pr, rows, :]
    edge = b[0:1, :] if reverse else b[c - 1:c, :]
    q_in = (q * jnp.exp(b)).astype(BF16)
    k_st = (k * jnp.exp(edge - b)).astype(BF16)
    st = st_ref[pr]
    o_inter = lax.dot_general(q_in, st.astype(BF16), (((1,), (1,)), ((), ())),
                              preferred_element_type=F32)
    upd = lax.dot_general(v.astype(BF16), k_st, (((0,), (0,)), ((), ())),
                          preferred_element_type=F32)
    st_ref[pr] = st * jnp.exp(edge) + upd * bd_ref[...]
    t_idx = lax.broadcasted_iota(jnp.int32, (c, PAIR), 0)
    terms = []
    for s in range(c):
        ok = (t_idx <= s) if reverse else (t_idx >= s)
        e = jnp.exp(jnp.where(ok, b - b[s:s + 1, :], MASK_VALUE))
        terms.append(q * e * k[s:s + 1, :])
    a = jnp.concatenate(terms, axis=0).astype(BF16)
    sc = jnp.dot(a, ones_ref[...], preferred_element_type=F32)
    o = o_inter
    for s in range(c):
        o = o + sc[s * c:(s + 1) * c, :] * v[s:s + 1, :]
    o_ref[pr, rows, :] = o


def _hgrn_kernel(qf_ref, kf_ref, lf_ref, vf_ref, qb_ref, kb_ref, lb_ref, vb_ref,
                 tril_ref, triu_ref, ones_ref, bd_ref,
                 of_ref, ob_ref, stf_ref, stb_ref, bf_ref, bb_ref, *, tt):
    @pl.when(pl.program_id(1) == 0)
    def _():
        stf_ref[...] = jnp.zeros_like(stf_ref)
        stb_ref[...] = jnp.zeros_like(stb_ref)

    for pr in range(N_PAIRS):
        bf_ref[pr] = _cumsum_rows(tril_ref[...], lf_ref[pr])
        bb_ref[pr] = _cumsum_rows(triu_ref[...], lb_ref[pr])

    def body(ci, carry):
        rf = pl.multiple_of(ci * REC_CHUNK, REC_CHUNK)
        rb = pl.multiple_of(tt - REC_CHUNK - ci * REC_CHUNK, REC_CHUNK)
        for pr in range(N_PAIRS):
            _rec_chunk(qf_ref, kf_ref, vf_ref, bf_ref, stf_ref, of_ref, ones_ref, bd_ref, pr, rf, False)
            _rec_chunk(qb_ref, kb_ref, vb_ref, bb_ref, stb_ref, ob_ref, ones_ref, bd_ref, pr, rb, True)
        return carry

    lax.fori_loop(0, tt // REC_CHUNK, body, 0)


def _hgrn(qr, kf, lf, kb, lb, vr, bsz, seq, tt):
    n = qr.shape[1]
    nt = seq // tt
    fwd = pl.BlockSpec((N_PAIRS, tt, PAIR), lambda b, i: (0, b * nt + i, 0))
    bwd = pl.BlockSpec((N_PAIRS, tt, PAIR), lambda b, i: (0, b * nt + nt - 1 - i, 0))
    blk = np.arange(tt) // REC_CHUNK
    same = blk[:, None] == blk[None, :]
    idx = np.arange(tt)
    tril = jnp.asarray(same & (idx[:, None] >= idx[None, :]), BF16)
    triu = jnp.asarray(same & (idx[:, None] <= idx[None, :]), BF16)
    head = np.arange(PAIR) // HEAD_DIM
    bd = head[:, None] == head[None, :]
    shape = jax.ShapeDtypeStruct((N_PAIRS, n, PAIR), F32)
    return pl.pallas_call(
        functools.partial(_hgrn_kernel, tt=tt),
        grid=(bsz, nt),
        in_specs=[fwd, fwd, fwd, fwd, bwd, bwd, bwd, bwd,
                  _const_spec((tt, tt)), _const_spec((tt, tt)),
                  _const_spec((PAIR, PAIR)), _const_spec((PAIR, PAIR))],
        out_specs=[fwd, bwd],
        out_shape=[shape, shape],
        scratch_shapes=[pltpu.VMEM((N_PAIRS, PAIR, PAIR), F32)] * 2
                       + [pltpu.VMEM((N_PAIRS, tt, PAIR), F32)] * 2,
        compiler_params=_params(("arbitrary", "arbitrary")),
        name="hgrn2",
    )(qr, kf, lf, vr, qr, kb, lb, vr, tril, triu, jnp.asarray(bd, BF16), jnp.asarray(bd, F32))


def _outproj_kernel(x_ref, mod_ref, a_ref, o1_ref, o2_ref, o3_ref, l1_ref, l2_ref, l3_ref,
                    of_ref, ob_ref, gr_ref, rw_ref, w_ref, y_ref):
    acc = jnp.dot(a_ref[...], w_ref[0:CONV_CH, :], preferred_element_type=F32)

    l1, l2, l3 = l1_ref[...], l2_ref[...], l3_ref[...]
    m = jnp.maximum(jnp.maximum(l1, l2), l3)
    e1, e2, e3 = jnp.exp(l1 - m), jnp.exp(l2 - m), jnp.exp(l3 - m)
    att = (e1 * o1_ref[...] + e2 * o2_ref[...] + e3 * o3_ref[...]) / (e1 + e2 + e3)
    acc += jnp.dot(att.astype(BF16), w_ref[CONV_CH:CONV_CH + ATT_WIDTH, :], preferred_element_type=F32)

    first = lax.broadcasted_iota(jnp.int32, (1, PAIR), 1) < HEAD_DIM
    base = CONV_CH + ATT_WIDTH
    for pr in range(N_PAIRS):
        o = of_ref[pr] + ob_ref[pr]
        sq = o * o
        s0 = jnp.sum(jnp.where(first, sq, 0.0), axis=-1, keepdims=True)
        s1 = jnp.sum(jnp.where(first, 0.0, sq), axis=-1, keepdims=True)
        ms = jnp.where(first, s0, s1) * (1.0 / HEAD_DIM)
        rec = o * lax.rsqrt(ms + EPS) * rw_ref[:, pr * PAIR:(pr + 1) * PAIR] * gr_ref[pr]
        acc += jnp.dot(rec.astype(BF16), w_ref[base + pr * PAIR:base + (pr + 1) * PAIR, :],
                       preferred_element_type=F32)
    y_ref[...] = x_ref[...] + mod_ref[0][2:3, :] * acc


def _outproj(x2, mod_l, a, outs, lses, o_f, o_b, gr, rec_norm_w, w_out_bf, seq, tm):
    n, d = x2.shape
    tiles_per_seq = seq // tm
    row = lambda i: (i, 0)
    att = pl.BlockSpec((tm, ATT_WIDTH), row)
    rec = pl.BlockSpec((N_PAIRS, tm, PAIR), lambda i: (0, i, 0))
    return pl.pallas_call(
        _outproj_kernel,
        grid=(n // tm,),
        in_specs=[pl.BlockSpec((tm, d), row),
                  pl.BlockSpec((1, N_MOD, d), lambda i: (i // tiles_per_seq, 0, 0)),
                  pl.BlockSpec((tm, CONV_CH), row)] + [att] * 6 + [rec] * 3
                 + [_const_spec((1, REC_WIDTH)), _const_spec((d, d))],
        out_specs=pl.BlockSpec((tm, d), row),
        out_shape=jax.ShapeDtypeStruct((n, d), F32),
        compiler_params=_params(("arbitrary",)),
        name="outproj",
    )(x2, mod_l, a, *outs, *lses, o_f, o_b, gr, rec_norm_w.reshape(1, REC_WIDTH), w_out_bf)


FFN_HALO = 16
FFN_TILE = 256


def _ffn_kernel(x_ref, xp_ref, xn_ref, mod_ref, nw_ref, wup_ref, cw_ref, wd_ref, fw_ref,
                y_ref, hx_ref, acc_ref, *, seq, tm, final):
    i = pl.program_id(0)
    t0 = (i * tm) % seq
    mod = mod_ref[0]

    def pre(xv):
        return (_rms(xv, nw_ref[...]) * (1.0 + mod[4:5, :]) + mod[3:4, :]).astype(BF16)

    x = x_ref[...]
    zero = jnp.zeros((FFN_HALO, D_MODEL), BF16)
    hx_ref[0:FFN_HALO] = jnp.where(t0 == 0, zero, pre(xp_ref[...]))
    hx_ref[FFN_HALO:FFN_HALO + tm] = pre(x)
    hx_ref[FFN_HALO + tm:] = jnp.where(t0 + tm == seq, zero, pre(xn_ref[...]))
    hx = hx_ref[...]

    def conv(u, w):
        lo, hi = FFN_HALO, FFN_HALO + tm
        return (w[0:1, :] * u[lo - 1:hi - 1, :] + w[1:2, :] * u[lo:hi, :] + w[2:3, :] * u[lo + 1:hi + 1, :])

    for c in range(D_FF // FFN_TILE):
        gs = slice(c * FFN_TILE, (c + 1) * FFN_TILE)
        vs = slice(D_FF + c * FFN_TILE, D_FF + (c + 1) * FFN_TILE)
        gate = conv(jnp.dot(hx, wup_ref[:, gs], preferred_element_type=F32), cw_ref[:, gs])
        val = conv(jnp.dot(hx, wup_ref[:, vs], preferred_element_type=F32), cw_ref[:, vs])
        act = (0.5 * gate * (1.0 + lax.erf(gate * (2.0 ** -0.5))) * val).astype(BF16)
        part = jnp.dot(act, wd_ref[gs, :], preferred_element_type=F32)
        if c == 0:
            acc_ref[...] = part
        else:
            acc_ref[...] += part
    y = x + mod[5:6, :] * acc_ref[...]
    if final:
        y = _rms(y, fw_ref[...])
    y_ref[...] = y


def _ffn(x2, mod_l, norm_w, w_up_bf, conv_w, w_down_bf, final_w, final, seq, tm):
    n, d = x2.shape
    tiles_per_seq = seq // tm
    hb = tm // FFN_HALO
    nhb = n // FFN_HALO
    return pl.pallas_call(
        functools.partial(_ffn_kernel, seq=seq, tm=tm, final=final),
        grid=(n // tm,),
        in_specs=[pl.BlockSpec((tm, d), lambda i: (i, 0)),
                  pl.BlockSpec((FFN_HALO, d), lambda i: (jnp.maximum(i * hb - 1, 0), 0)),
                  pl.BlockSpec((FFN_HALO, d), lambda i: (jnp.minimum((i + 1) * hb, nhb - 1), 0)),
                  pl.BlockSpec((1, N_MOD, d), lambda i: (i // tiles_per_seq, 0, 0)),
                  _const_spec((1, d)), _const_spec((d, 2 * D_FF)), _const_spec((3, 2 * D_FF)),
                  _const_spec((D_FF, d)), _const_spec((1, d))],
        out_specs=pl.BlockSpec((tm, d), lambda i: (i, 0)),
        out_shape=jax.ShapeDtypeStruct((n, d), F32),
        scratch_shapes=[pltpu.VMEM((tm + 2 * FFN_HALO, d), BF16), pltpu.VMEM((tm, d), F32)],
        compiler_params=_params(("arbitrary",)),
        name="conv_ffn",
    )(x2, x2, x2, mod_l, norm_w.reshape(1, d), w_up_bf, conv_w, w_down_bf, final_w.reshape(1, d))


def kernel(x, c, w_ada, b_ada, norm1_w, w_in, conv_a_w, conv_a_b, ln_a_w, ln_a_b, lb_gamma,
           rec_norm_w, w_out, norm2_w, w_up, conv_f_w, w_down, final_norm_w):
    bsz, seq, d = x.shape
    depth = w_in.shape[0]
    n = bsz * seq
    tm = min(512, seq)
    tt = min(256, seq)
    mod = _modulation(c, w_ada, b_ada)
    x2 = x.reshape(n, d)
    for l in range(depth):
        (a, q, k, v, qr, kf, lf, kb, lb, vr, gr) = _inproj(
            x2, mod[l], norm1_w[l], w_in[l].astype(BF16), lb_gamma, l, seq, tm)
        a = _conv_a(a, conv_a_w[l], conv_a_b[l], ln_a_w[l], ln_a_b[l], seq, tm)
        branches = [_attn_branch(q, k, v, bsz, seq, dil) for dil in DILATIONS]
        o_f, o_b = _hgrn(qr, kf, lf, kb, lb, vr, bsz, seq, tt)
        x2 = _outproj(x2, mod[l], a, [o for o, _ in branches], [s for _, s in branches],
                      o_f, o_b, gr, rec_norm_w[l], w_out[l].astype(BF16), seq, tm)
        x2 = _ffn(x2, mod[l], norm2_w[l], w_up[l].astype(BF16), conv_f_w[l], w_down[l].astype(BF16),
                  final_norm_w, l == depth - 1, seq, tm)
    return x2.reshape(bsz, seq, d)
```

```python
import functools

import numpy as np
import jax
import jax.numpy as jnp
from jax import lax
from jax.experimental import pallas as pl
from jax.experimental.pallas import tpu as pltpu

F32 = jnp.float32
BF16 = jnp.bfloat16

D_MODEL = 1024
HEAD_DIM = 64
CONV_CH = 256
CONV_WIDTH = 31
ATT_WIDTH = 384
N_ATT_HEADS = 6
DILATIONS = (1, 4, 16)
BAND_HALF = 64
ATT_BLOCK = 128
ALIBI_MAX_EXP = 8.0
MASK_VALUE = -1e30
REC_WIDTH = 384
F_TINY = 1e-30
D_FF = 2816
N_MOD = 6
EPS = 1e-6
IN_COLS = 3584

LANES = 128
PAIR = 2 * HEAD_DIM
N_PAIRS = REC_WIDTH // PAIR
REC_CHUNK = 16
REC_FAST = 64
FAST_LIMIT = 60.0
VMEM_LIMIT = 56 * 1024 * 1024


def _params(sem, vmem=VMEM_LIMIT):
    return pltpu.CompilerParams(dimension_semantics=sem, vmem_limit_bytes=vmem)


def _const_spec(shape):
    nd = len(shape)
    return pl.BlockSpec(shape, lambda *_: (0,) * nd, pipeline_mode=pl.Buffered(1))


def _sigmoids(z):
    e = jnp.exp(-jnp.abs(z))
    r = 1.0 / (1.0 + e)
    er = e * r
    pos = z >= 0
    return jnp.where(pos, r, er), jnp.where(pos, er, r)


def _silu(z):
    return z * _sigmoids(z)[0]


def _rms(x, w):
    ms = jnp.mean(x * x, axis=-1, keepdims=True)
    return x * lax.rsqrt(ms + EPS) * w


def _mod_kernel(c_ref, w_ref, b_ref, o_ref):
    cond = _silu(c_ref[...])
    o_ref[0] = jnp.dot(cond, w_ref[0], preferred_element_type=F32) + b_ref[0]


def _modulation(c, w_ada, b_ada):
    depth, d, n = w_ada.shape
    bsz = c.shape[0]
    bp = -(-bsz // 8) * 8
    cp = jnp.zeros((bp, d), F32).at[:bsz].set(c)
    tn = 1536
    out = pl.pallas_call(
        _mod_kernel,
        grid=(depth, n // tn),
        in_specs=[pl.BlockSpec((bp, d), lambda l, j: (0, 0)),
                  pl.BlockSpec((1, d, tn), lambda l, j: (l, 0, j)),
                  pl.BlockSpec((1, 1, tn), lambda l, j: (l, 0, j))],
        out_specs=pl.BlockSpec((1, bp, tn), lambda l, j: (l, 0, j)),
        out_shape=jax.ShapeDtypeStruct((depth, bp, n), F32),
        compiler_params=_params(("arbitrary", "arbitrary")),
        name="adaln_mod",
    )(cp, w_ada, b_ada.reshape(depth, 1, n))
    return out[:, :bsz].reshape(depth, bsz, N_MOD, d)


def _inproj_kernel(x_ref, mod_ref, nw_ref, w_ref, lbg_ref,
                   a_ref, q_ref, k_ref, v_ref,
                   qr_ref, kf_ref, lf_ref, kb_ref, lb_ref, vr_ref, gr_ref, *, layer):
    mod = mod_ref[0]
    h = (_rms(x_ref[...], nw_ref[...]) * (1.0 + mod[1:2, :]) + mod[0:1, :]).astype(BF16)

    def proj(c0, n):
        return jnp.dot(h, w_ref[:, c0:c0 + n], preferred_element_type=F32)

    ag = proj(0, 2 * CONV_CH)
    a_ref[...] = ag[:, :CONV_CH] * _sigmoids(ag[:, CONV_CH:])[0]
    c0 = 2 * CONV_CH
    q_ref[...] = (proj(c0, ATT_WIDTH) * (HEAD_DIM ** -0.5)).astype(BF16)
    k_ref[...] = proj(c0 + ATT_WIDTH, ATT_WIDTH).astype(BF16)
    v_ref[...] = proj(c0 + 2 * ATT_WIDTH, ATT_WIDTH).astype(BF16)
    c0 += 3 * ATT_WIDTH

    g = lbg_ref[...]
    ge = jnp.exp(g - jnp.max(g, axis=0, keepdims=True))
    p = ge / jnp.sum(ge, axis=0, keepdims=True)
    lbv = jnp.zeros(p.shape[1:], F32)
    for j in range(1, layer + 1):
        lbv = lbv + p[j]

    zq = proj(c0, REC_WIDTH)
    zf = proj(c0 + REC_WIDTH, REC_WIDTH)
    zb = proj(c0 + 2 * REC_WIDTH, REC_WIDTH)
    zi = proj(c0 + 3 * REC_WIDTH, REC_WIDTH)
    zg = proj(c0 + 4 * REC_WIDTH, REC_WIDTH)

    def gate(z, lb):
        sp, sn = _sigmoids(z)
        f = lb + (1.0 - lb) * sp
        return jnp.log(jnp.maximum(f, F_TINY)), (1.0 - lb) * sn

    qs = _silu(zq)
    gs = _silu(zg)
    lff, kff = gate(zf, lbv[0:1, :])
    lbb, kbb = gate(zb, lbv[1:2, :])
    for pr in range(N_PAIRS):
        sl = slice(pr * PAIR, (pr + 1) * PAIR)
        qr_ref[pr] = qs[:, sl]
        gr_ref[pr] = gs[:, sl]
        vr_ref[pr] = zi[:, sl]
        lf_ref[pr] = lff[:, sl]
        kf_ref[pr] = kff[:, sl]
        lb_ref[pr] = lbb[:, sl]
        kb_ref[pr] = kbb[:, sl]


def _inproj(x2, mod_l, norm_w, w_in_bf, lb_gamma, layer, seq, tm):
    n, d = x2.shape
    tiles_per_seq = seq // tm
    row = lambda i: (i, 0)
    rec_spec = pl.BlockSpec((N_PAIRS, tm, PAIR), lambda i: (0, i, 0))
    rec_shape = jax.ShapeDtypeStruct((N_PAIRS, n, PAIR), F32)
    return pl.pallas_call(
        functools.partial(_inproj_kernel, layer=layer),
        grid=(n // tm,),
        in_specs=[pl.BlockSpec((tm, d), row),
                  pl.BlockSpec((1, N_MOD, d), lambda i: (i // tiles_per_seq, 0, 0)),
                  _const_spec((1, d)),
                  _const_spec((d, IN_COLS)),
                  _const_spec(lb_gamma.shape)],
        out_specs=[pl.BlockSpec((tm, CONV_CH), row)] + [pl.BlockSpec((tm, ATT_WIDTH), row)] * 3
                  + [rec_spec] * 7,
        out_shape=[jax.ShapeDtypeStruct((n, CONV_CH), F32)]
                  + [jax.ShapeDtypeStruct((n, ATT_WIDTH), BF16)] * 3 + [rec_shape] * 7,
        compiler_params=_params(("arbitrary",)),
        name="inproj",
    )(x2, mod_l, norm_w.reshape(1, d), w_in_bf, lb_gamma)


CONV_HALO = 16
CONV_ROWS = 64


def _conva_kernel(a_ref, ap_ref, an_ref, w_ref, b_ref, lw_ref, lb_ref, o_ref, ext_ref, y_ref, *, seq, tm):
    i = pl.program_id(0)
    t0 = (i * tm) % seq
    groups = CONV_CH // LANES
    prev = jnp.where(t0 == 0, 0.0, ap_ref[...])
    nxt = jnp.where(t0 + tm == seq, 0.0, an_ref[...])
    for g in range(groups):
        cols = slice(g * LANES, (g + 1) * LANES)
        ext_ref[g, 0:CONV_HALO] = prev[:, cols]
        ext_ref[g, CONV_HALO:CONV_HALO + tm] = a_ref[:, cols]
        ext_ref[g, CONV_HALO + tm:] = nxt[:, cols]
    w = w_ref[...]
    shift = CONV_HALO - CONV_WIDTH // 2
    for r0 in range(0, tm, 2 * CONV_ROWS):
        for parity in range(2):
            acc = jnp.zeros((CONV_ROWS, CONV_CH), F32) + b_ref[...]
            for j in range(CONV_WIDTH):
                rows = pl.ds(r0 + parity + j + shift, CONV_ROWS, stride=2)
                tap = jnp.concatenate([ext_ref[g, rows, :] for g in range(groups)], axis=-1)
                acc = acc + w[j:j + 1, :] * tap
            mu = jnp.mean(acc, axis=-1, keepdims=True)
            cen = acc - mu
            var = jnp.mean(cen * cen, axis=-1, keepdims=True)
            y = _silu(cen * lax.rsqrt(var + EPS) * lw_ref[...] + lb_ref[...])
            for g in range(groups):
                y_ref[g, pl.ds(r0 + parity, CONV_ROWS, stride=2), :] = y[:, g * LANES:(g + 1) * LANES]
    o_ref[...] = jnp.concatenate([y_ref[g] for g in range(groups)], axis=-1).astype(BF16)


def _conv_a(a, conv_w, conv_b, ln_w, ln_b, seq, tm):
    n, ch = a.shape
    hb = tm // CONV_HALO
    nhb = n // CONV_HALO
    vec = lambda v: v.reshape(1, ch)
    return pl.pallas_call(
        functools.partial(_conva_kernel, seq=seq, tm=tm),
        grid=(n // tm,),
        in_specs=[pl.BlockSpec((tm, ch), lambda i: (i, 0)),
                  pl.BlockSpec((CONV_HALO, ch), lambda i: (jnp.maximum(i * hb - 1, 0), 0)),
                  pl.BlockSpec((CONV_HALO, ch), lambda i: (jnp.minimum((i + 1) * hb, nhb - 1), 0)),
                  _const_spec((CONV_WIDTH, ch)), _const_spec((1, ch)),
                  _const_spec((1, ch)), _const_spec((1, ch))],
        out_specs=pl.BlockSpec((tm, ch), lambda i: (i, 0)),
        out_shape=jax.ShapeDtypeStruct((n, ch), BF16),
        scratch_shapes=[pltpu.VMEM((ch // LANES, tm + 2 * CONV_HALO, LANES), F32),
                        pltpu.VMEM((ch // LANES, tm, LANES), F32)],
        compiler_params=_params(("arbitrary",)),
        name="conv_a",
    )(a, a, a, conv_w, vec(conv_b), vec(ln_w), vec(ln_b))


def _attn_kernel(q_ref, k_ref, kp_ref, kn_ref, v_ref, vp_ref, vn_ref, bias_ref,
                 o_ref, l_ref, kx_ref, vx_ref, *, tq, sub_len):
    i = pl.program_id(1)
    kx_ref[0:BAND_HALF] = kp_ref[0]
    kx_ref[BAND_HALF:BAND_HALF + tq] = k_ref[0]
    kx_ref[BAND_HALF + tq:] = kn_ref[0]
    vx_ref[0:BAND_HALF] = vp_ref[0]
    vx_ref[BAND_HALF:BAND_HALF + tq] = v_ref[0]
    vx_ref[BAND_HALF + tq:] = vn_ref[0]
    span = ATT_BLOCK + 2 * BAND_HALF
    first = lax.broadcasted_iota(jnp.int32, (ATT_BLOCK, PAIR), 1) < HEAD_DIM
    col = lax.broadcasted_iota(jnp.int32, (1, span), 1)
    nt = (((1,), (1,)), ((), ()))
    for blk in range(tq // ATT_BLOCK):
        r0 = blk * ATT_BLOCK
        kpos = i * tq + (r0 - BAND_HALF) + col
        valid = (kpos >= 0) & (kpos < sub_len)
        for pr in range(N_PAIRS):
            sl = slice(pr * PAIR, (pr + 1) * PAIR)
            qp = q_ref[0, r0:r0 + ATT_BLOCK, sl]
            kp = kx_ref[r0:r0 + span, sl]
            vp = vx_ref[r0:r0 + span, sl]
            res = []
            for hh in range(2):
                keep = first if hh == 0 else jnp.logical_not(first)
                qm = jnp.where(keep, qp, jnp.zeros_like(qp))
                s = lax.dot_general(qm, kp, nt, preferred_element_type=F32)
                s = jnp.where(valid, s + bias_ref[2 * pr + hh], MASK_VALUE)
                m = jnp.max(s, axis=-1, keepdims=True)
                pe = jnp.exp(s - m)
                den = jnp.sum(pe, axis=-1, keepdims=True)
                o = jnp.dot(pe.astype(BF16), vp, preferred_element_type=F32)
                res.append((o / den, m + jnp.log(den)))
            o_ref[0, r0:r0 + ATT_BLOCK, sl] = jnp.where(first, res[0][0], res[1][0])
            l_ref[0, r0:r0 + ATT_BLOCK, sl] = jnp.where(first, res[0][1], res[1][1])


def _attn_bias(dilation):
    slopes = 2.0 ** (-ALIBI_MAX_EXP * np.arange(1, N_ATT_HEADS + 1) / N_ATT_HEADS)
    span = ATT_BLOCK + 2 * BAND_HALF
    rel = np.arange(span)[None, :] - BAND_HALF - np.arange(ATT_BLOCK)[:, None]
    dist = (np.abs(rel) * dilation).astype(np.float32)
    bias = -(slopes.astype(np.float32)[:, None, None] * dist[None])
    bias = np.where((np.abs(rel) <= BAND_HALF)[None], bias, np.float32(MASK_VALUE))
    return jnp.asarray(bias, F32)


def _attn_branch(q, k, v, bsz, seq, dilation):
    n = q.shape[0]
    sub_len = seq // dilation
    bd = bsz * dilation

    def to_sub(t):
        if dilation == 1:
            return t.reshape(bd, sub_len, ATT_WIDTH)
        return t.reshape(bsz, sub_len, dilation, ATT_WIDTH).transpose(0, 2, 1, 3).reshape(bd, sub_len, ATT_WIDTH)

    def from_sub(t):
        if dilation == 1:
            return t.reshape(n, ATT_WIDTH)
        return t.reshape(bsz, dilation, sub_len, ATT_WIDTH).transpose(0, 2, 1, 3).reshape(n, ATT_WIDTH)

    tq = min(sub_len, 1024)
    hb = tq // BAND_HALF
    nhb = sub_len // BAND_HALF
    main = pl.BlockSpec((1, tq, ATT_WIDTH), lambda b, i: (b, i, 0))
    prev = pl.BlockSpec((1, BAND_HALF, ATT_WIDTH), lambda b, i: (b, jnp.maximum(i * hb - 1, 0), 0))
    nxt = pl.BlockSpec((1, BAND_HALF, ATT_WIDTH), lambda b, i: (b, jnp.minimum((i + 1) * hb, nhb - 1), 0))
    qs, ks, vs = to_sub(q), to_sub(k), to_sub(v)
    span = ATT_BLOCK + 2 * BAND_HALF
    o, lse = pl.pallas_call(
        functools.partial(_attn_kernel, tq=tq, sub_len=sub_len),
        grid=(bd, sub_len // tq),
        in_specs=[main, main, prev, nxt, main, prev, nxt,
                  _const_spec((N_ATT_HEADS, ATT_BLOCK, span))],
        out_specs=[main, main],
        out_shape=[jax.ShapeDtypeStruct((bd, sub_len, ATT_WIDTH), F32)] * 2,
        scratch_shapes=[pltpu.VMEM((tq + 2 * BAND_HALF, ATT_WIDTH), BF16)] * 2,
        compiler_params=_params(("arbitrary", "arbitrary")),
        name=f"attn_d{dilation}",
    )(qs, ks, ks, ks, vs, vs, vs, _attn_bias(dilation))
    return from_sub(o), from_sub(lse)


def _cumsum_rows(tri, x):
    hi = x.astype(BF16)
    r1 = x - hi.astype(F32)
    mid = r1.astype(BF16)
    lo = (r1 - mid.astype(F32)).astype(BF16)
    dot = lambda t: jnp.dot(tri, t, preferred_element_type=F32)
    return dot(hi) + dot(mid) + dot(lo)


def _rec_chunk(q_ref, k_ref, v_ref, b_ref, st_ref, o_ref, ones_ref, bd_ref, pr, r, reverse):
    c = REC_CHUNK
    rows = pl.ds(r, c)
    q = q_ref[pr, rows, :]
    k = k_ref[pr, rows, :]
    v = v_ref[pr, rows, :]
    b = b_ref[pr, rows, :]
    edge = b[0:1, :] if reverse else b[c - 1:c, :]
    q_in = (q * jnp.exp(b)).astype(BF16)
    k_st = (k * jnp.exp(edge - b)).astype(BF16)
    st = st_ref[pr]
    o_inter = lax.dot_general(q_in, st.astype(BF16), (((1,), (1,)), ((), ())),
                              preferred_element_type=F32)
    upd = lax.dot_general(v.astype(BF16), k_st, (((0,), (0,)), ((), ())),
                          preferred_element_type=F32)
    st_ref[pr] = st * jnp.exp(edge) + upd * bd_ref[...]
    t_idx = lax.broadcasted_iota(jnp.int32, (c, PAIR), 0)
    terms = []
    for s in range(c):
        ok = (t_idx <= s) if reverse else (t_idx >= s)
        e = jnp.exp(jnp.where(ok, b - b[s:s + 1, :], MASK_VALUE))
        terms.append(q * e * k[s:s + 1, :])
    a = jnp.concatenate(terms, axis=0).astype(BF16)
    sc = jnp.dot(a, ones_ref[...], preferred_element_type=F32)
    o = o_inter
    for s in range(c):
        o = o + sc[s * c:(s + 1) * c, :] * v[s:s + 1, :]
    o_ref[pr, rows, :] = o


def _rec_fast_chunk(q_ref, k_ref, v_ref, b_ref, st_ref, o_ref, bd_ref, tri, pr, r0, reverse):
    c = REC_FAST
    rows = slice(r0, r0 + c)
    q = q_ref[pr, rows, :]
    k = k_ref[pr, rows, :]
    vb = v_ref[pr, rows, :].astype(BF16)
    b = b_ref[pr, rows, :]
    mid = b[c // 2:c // 2 + 1, :]
    edge = b[0:1, :] if reverse else b[c - 1:c, :]
    qt = q * jnp.exp(b - mid)
    kt = k * jnp.exp(mid - b)
    q_in = (qt * jnp.exp(mid)).astype(BF16)
    k_st = (kt * jnp.exp(edge - mid)).astype(BF16)
    ktb = kt.astype(BF16)
    first = lax.broadcasted_iota(jnp.int32, (c, PAIR), 1) < HEAD_DIM
    zero = jnp.zeros_like(ktb)
    k2 = jnp.concatenate([jnp.where(first, ktb, zero), jnp.where(first, zero, ktb)], axis=0)
    v2 = jnp.concatenate([jnp.where(first, vb, zero), jnp.where(first, zero, vb)], axis=0)
    nt = (((1,), (1,)), ((), ()))
    s = lax.dot_general(qt.astype(BF16), k2, nt, preferred_element_type=F32)
    p = jnp.where(tri, s, 0.0).astype(BF16)
    st = st_ref[pr]
    o = (jnp.dot(p, v2, preferred_element_type=F32)
         + lax.dot_general(q_in, st.astype(BF16), nt, preferred_element_type=F32))
    upd = lax.dot_general(vb, k_st, (((0,), (0,)), ((), ())), preferred_element_type=F32)
    st_ref[pr] = st * jnp.exp(edge) + upd * bd_ref[...]
    o_ref[pr, rows, :] = o


def _hgrn_kernel(qf_ref, kf_ref, lf_ref, vf_ref, qb_ref, kb_ref, lb_ref, vb_ref,
                 tril_ref, triu_ref, fl_ref, fu_ref, ones_ref, bd_ref,
                 of_ref, ob_ref, stf_ref, stb_ref, bf_ref, bb_ref, *, tt):
    @pl.when(pl.program_id(1) == 0)
    def _():
        stf_ref[...] = jnp.zeros_like(stf_ref)
        stb_ref[...] = jnp.zeros_like(stb_ref)

    c = REC_FAST
    dev = jnp.zeros((c, PAIR), F32)
    for r0 in range(0, tt, c):
        for pr in range(N_PAIRS):
            for l_ref, t_ref, b_ref in ((lf_ref, fl_ref, bf_ref), (lb_ref, fu_ref, bb_ref)):
                b = _cumsum_rows(t_ref[...], l_ref[pr, r0:r0 + c, :])
                b_ref[pr, r0:r0 + c, :] = b
                dev = jnp.maximum(dev, jnp.abs(b - b[c // 2:c // 2 + 1, :]))
    safe = jnp.max(dev) <= FAST_LIMIT

    @pl.when(safe)
    def _():
        row = lax.broadcasted_iota(jnp.int32, (c, PAIR), 0)
        col = lax.broadcasted_iota(jnp.int32, (c, PAIR), 1) % c
        lower, upper = row >= col, row <= col
        for r0 in range(0, tt, c):
            for pr in range(N_PAIRS):
                _rec_fast_chunk(qf_ref, kf_ref, vf_ref, bf_ref, stf_ref, of_ref, bd_ref, lower, pr, r0, False)
                _rec_fast_chunk(qb_ref, kb_ref, vb_ref, bb_ref, stb_ref, ob_ref, bd_ref, upper, pr,
                                tt - c - r0, True)

    @pl.when(jnp.logical_not(safe))
    def _():
        for pr in range(N_PAIRS):
            bf_ref[pr] = _cumsum_rows(tril_ref[...], lf_ref[pr])
            bb_ref[pr] = _cumsum_rows(triu_ref[...], lb_ref[pr])

        def body(ci, carry):
            rf = pl.multiple_of(ci * REC_CHUNK, REC_CHUNK)
            rb = pl.multiple_of(tt - REC_CHUNK - ci * REC_CHUNK, REC_CHUNK)
            for pr in range(N_PAIRS):
                _rec_chunk(qf_ref, kf_ref, vf_ref, bf_ref, stf_ref, of_ref, ones_ref, bd_ref, pr, rf, False)
                _rec_chunk(qb_ref, kb_ref, vb_ref, bb_ref, stb_ref, ob_ref, ones_ref, bd_ref, pr, rb, True)
            return carry

        lax.fori_loop(0, tt // REC_CHUNK, body, 0)


def _tri_pair(size, block):
    idx = np.arange(size)
    same = (idx[:, None] // block) == (idx[None, :] // block)
    lower = same & (idx[:, None] >= idx[None, :])
    upper = same & (idx[:, None] <= idx[None, :])
    return jnp.asarray(lower, BF16), jnp.asarray(upper, BF16)


def _hgrn(qr, kf, lf, kb, lb, vr, bsz, seq, tt):
    n = qr.shape[1]
    nt = seq // tt
    fwd = pl.BlockSpec((N_PAIRS, tt, PAIR), lambda b, i: (0, b * nt + i, 0))
    bwd = pl.BlockSpec((N_PAIRS, tt, PAIR), lambda b, i: (0, b * nt + nt - 1 - i, 0))
    tril, triu = _tri_pair(tt, REC_CHUNK)
    fl, fu = _tri_pair(REC_FAST, REC_FAST)
    head = np.arange(PAIR) // HEAD_DIM
    bd = head[:, None] == head[None, :]
    shape = jax.ShapeDtypeStruct((N_PAIRS, n, PAIR), F32)
    return pl.pallas_call(
        functools.partial(_hgrn_kernel, tt=tt),
        grid=(bsz, nt),
        in_specs=[fwd, fwd, fwd, fwd, bwd, bwd, bwd, bwd,
                  _const_spec((tt, tt)), _const_spec((tt, tt)),
                  _const_spec((REC_FAST, REC_FAST)), _const_spec((REC_FAST, REC_FAST)),
                  _const_spec((PAIR, PAIR)), _const_spec((PAIR, PAIR))],
        out_specs=[fwd, bwd],
        out_shape=[shape, shape],
        scratch_shapes=[pltpu.VMEM((N_PAIRS, PAIR, PAIR), F32)] * 2
                       + [pltpu.VMEM((N_PAIRS, tt, PAIR), F32)] * 2,
        compiler_params=_params(("arbitrary", "arbitrary")),
        name="hgrn2",
    )(qr, kf, lf, vr, qr, kb, lb, vr, tril, triu, fl, fu, jnp.asarray(bd, BF16), jnp.asarray(bd, F32))


def _outproj_kernel(x_ref, mod_ref, a_ref, o1_ref, o2_ref, o3_ref, l1_ref, l2_ref, l3_ref,
                    of_ref, ob_ref, gr_ref, rw_ref, w_ref, y_ref):
    acc = jnp.dot(a_ref[...], w_ref[0:CONV_CH, :], preferred_element_type=F32)

    l1, l2, l3 = l1_ref[...], l2_ref[...], l3_ref[...]
    m = jnp.maximum(jnp.maximum(l1, l2), l3)
    e1, e2, e3 = jnp.exp(l1 - m), jnp.exp(l2 - m), jnp.exp(l3 - m)
    att = (e1 * o1_ref[...] + e2 * o2_ref[...] + e3 * o3_ref[...]) / (e1 + e2 + e3)
    acc += jnp.dot(att.astype(BF16), w_ref[CONV_CH:CONV_CH + ATT_WIDTH, :], preferred_element_type=F32)

    first = lax.broadcasted_iota(jnp.int32, (1, PAIR), 1) < HEAD_DIM
    base = CONV_CH + ATT_WIDTH
    for pr in range(N_PAIRS):
        o = of_ref[pr] + ob_ref[pr]
        sq = o * o
        s0 = jnp.sum(jnp.where(first, sq, 0.0), axis=-1, keepdims=True)
        s1 = jnp.sum(jnp.where(first, 0.0, sq), axis=-1, keepdims=True)
        ms = jnp.where(first, s0, s1) * (1.0 / HEAD_DIM)
        rec = o * lax.rsqrt(ms + EPS) * rw_ref[:, pr * PAIR:(pr + 1) * PAIR] * gr_ref[pr]
        acc += jnp.dot(rec.astype(BF16), w_ref[base + pr * PAIR:base + (pr + 1) * PAIR, :],
                       preferred_element_type=F32)
    y_ref[...] = x_ref[...] + mod_ref[0][2:3, :] * acc


def _outproj(x2, mod_l, a, outs, lses, o_f, o_b, gr, rec_norm_w, w_out_bf, seq, tm):
    n, d = x2.shape
    tiles_per_seq = seq // tm
    row = lambda i: (i, 0)
    att = pl.BlockSpec((tm, ATT_WIDTH), row)
    rec = pl.BlockSpec((N_PAIRS, tm, PAIR), lambda i: (0, i, 0))
    return pl.pallas_call(
        _outproj_kernel,
        grid=(n // tm,),
        in_specs=[pl.BlockSpec((tm, d), row),
                  pl.BlockSpec((1, N_MOD, d), lambda i: (i // tiles_per_seq, 0, 0)),
                  pl.BlockSpec((tm, CONV_CH), row)] + [att] * 6 + [rec] * 3
                 + [_const_spec((1, REC_WIDTH)), _const_spec((d, d))],
        out_specs=pl.BlockSpec((tm, d), row),
        out_shape=jax.ShapeDtypeStruct((n, d), F32),
        compiler_params=_params(("arbitrary",)),
        name="outproj",
    )(x2, mod_l, a, *outs, *lses, o_f, o_b, gr, rec_norm_w.reshape(1, REC_WIDTH), w_out_bf)


FFN_HALO = 16
FFN_TILE = 256


def _ffn_kernel(x_ref, xp_ref, xn_ref, mod_ref, nw_ref, wup_ref, cw_ref, wd_ref, fw_ref,
                y_ref, hx_ref, acc_ref, u_ref, yi_ref, *, seq, tm, final):
    i = pl.program_id(0)
    t0 = (i * tm) % seq
    mod = mod_ref[0]

    def pre(xv):
        return (_rms(xv, nw_ref[...]) * (1.0 + mod[4:5, :]) + mod[3:4, :]).astype(BF16)

    zero = jnp.zeros((FFN_HALO, D_MODEL), BF16)
    hx_ref[0:FFN_HALO] = jnp.where(t0 == 0, zero, pre(xp_ref[...]))
    hx_ref[FFN_HALO:FFN_HALO + tm] = pre(x_ref[...])
    hx_ref[FFN_HALO + tm:] = jnp.where(t0 + tm == seq, zero, pre(xn_ref[...]))

    n_steps = D_FF // FFN_TILE
    half = tm // 2

    groups = FFN_TILE // LANES

    def up(c):
        hx = hx_ref[...]
        for part, base in enumerate((0, D_FF)):
            cols = slice(base + c * FFN_TILE, base + (c + 1) * FFN_TILE)
            u = jnp.dot(hx, wup_ref[:, cols], preferred_element_type=F32)
            for g in range(groups):
                u_ref[c % 2, part, g] = u[:, g * LANES:(g + 1) * LANES]

    def taps(c, part, parity):
        base = part * D_FF + c * FFN_TILE
        w = cw_ref[:, base:base + FFN_TILE]

        def row(off):
            rows = pl.ds(FFN_HALO + parity + off, half, stride=2)
            return jnp.concatenate([u_ref[c % 2, part, g, rows, :] for g in range(groups)], axis=-1)

        return w[0:1, :] * row(-1) + w[1:2, :] * row(0) + w[2:3, :] * row(1)

    up(0)
    for c in range(n_steps):
        if c + 1 < n_steps:
            up(c + 1)
        for parity in range(2):
            gate = taps(c, 0, parity)
            val = taps(c, 1, parity)
            act = (0.5 * gate * (1.0 + lax.erf(gate * (2.0 ** -0.5))) * val).astype(BF16)
            part = jnp.dot(act, wd_ref[c * FFN_TILE:(c + 1) * FFN_TILE, :], preferred_element_type=F32)
            if c == 0:
                acc_ref[parity] = part
            else:
                acc_ref[parity] += part
    for parity in range(2):
        for g in range(D_MODEL // LANES):
            yi_ref[g, pl.ds(parity, half, stride=2), :] = acc_ref[parity, :, g * LANES:(g + 1) * LANES]
    ffn = jnp.concatenate([yi_ref[g] for g in range(D_MODEL // LANES)], axis=-1)
    y = x_ref[...] + mod[5:6, :] * ffn
    if final:
        y = _rms(y, fw_ref[...])
    y_ref[...] = y


def _ffn(x2, mod_l, norm_w, w_up_bf, conv_w, w_down_bf, final_w, final, seq, tm):
    n, d = x2.shape
    tiles_per_seq = seq // tm
    hb = tm // FFN_HALO
    nhb = n // FFN_HALO
    return pl.pallas_call(
        functools.partial(_ffn_kernel, seq=seq, tm=tm, final=final),
        grid=(n // tm,),
        in_specs=[pl.BlockSpec((tm, d), lambda i: (i, 0)),
                  pl.BlockSpec((FFN_HALO, d), lambda i: (jnp.maximum(i * hb - 1, 0), 0)),
                  pl.BlockSpec((FFN_HALO, d), lambda i: (jnp.minimum((i + 1) * hb, nhb - 1), 0)),
                  pl.BlockSpec((1, N_MOD, d), lambda i: (i // tiles_per_seq, 0, 0)),
                  _const_spec((1, d)), _const_spec((d, 2 * D_FF)), _const_spec((3, 2 * D_FF)),
                  _const_spec((D_FF, d)), _const_spec((1, d))],
        out_specs=pl.BlockSpec((tm, d), lambda i: (i, 0)),
        out_shape=jax.ShapeDtypeStruct((n, d), F32),
        scratch_shapes=[pltpu.VMEM((tm + 2 * FFN_HALO, d), BF16), pltpu.VMEM((2, tm // 2, d), F32),
                        pltpu.VMEM((2, 2, FFN_TILE // LANES, tm + 2 * FFN_HALO, LANES), F32),
                        pltpu.VMEM((d // LANES, tm, LANES), F32)],
        compiler_params=_params(("arbitrary",)),
        name="conv_ffn",
    )(x2, x2, x2, mod_l, norm_w.reshape(1, d), w_up_bf, conv_w, w_down_bf, final_w.reshape(1, d))


def kernel(x, c, w_ada, b_ada, norm1_w, w_in, conv_a_w, conv_a_b, ln_a_w, ln_a_b, lb_gamma,
           rec_norm_w, w_out, norm2_w, w_up, conv_f_w, w_down, final_norm_w):
    bsz, seq, d = x.shape
    depth = w_in.shape[0]
    n = bsz * seq
    tm = min(512, seq)
    tm_ffn = min(1024, seq)
    tt = min(256, seq)
    mod = _modulation(c, w_ada, b_ada)
    x2 = x.reshape(n, d)
    for l in range(depth):
        (a, q, k, v, qr, kf, lf, kb, lb, vr, gr) = _inproj(
            x2, mod[l], norm1_w[l], w_in[l].astype(BF16), lb_gamma, l, seq, tm)
        a = _conv_a(a, conv_a_w[l], conv_a_b[l], ln_a_w[l], ln_a_b[l], seq, tm)
        branches = [_attn_branch(q, k, v, bsz, seq, dil) for dil in DILATIONS]
        o_f, o_b = _hgrn(qr, kf, lf, kb, lb, vr, bsz, seq, tt)
        x2 = _outproj(x2, mod[l], a, [o for o, _ in branches], [s for _, s in branches],
                      o_f, o_b, gr, rec_norm_w[l], w_out[l].astype(BF16), seq, tm)
        x2 = _ffn(x2, mod[l], norm2_w[l], w_up[l].astype(BF16), conv_f_w[l], w_down[l].astype(BF16),
                  final_norm_w, l == depth - 1, seq, tm_ffn)
    return x2.reshape(bsz, seq, d)
```

```python
import functools

import numpy as np
import jax
import jax.numpy as jnp
from jax import lax
from jax.experimental import pallas as pl
from jax.experimental.pallas import tpu as pltpu

F32 = jnp.float32
BF16 = jnp.bfloat16

D_MODEL = 1024
HEAD_DIM = 64
CONV_CH = 256
CONV_WIDTH = 31
ATT_WIDTH = 384
N_ATT_HEADS = 6
DILATIONS = (1, 4, 16)
BAND_HALF = 64
ATT_BLOCK = 128
ALIBI_MAX_EXP = 8.0
MASK_VALUE = -1e30
REC_WIDTH = 384
F_TINY = 1e-30
D_FF = 2816
N_MOD = 6
EPS = 1e-6
IN_COLS = 3584

LANES = 128
PAIR = 2 * HEAD_DIM
N_PAIRS = REC_WIDTH // PAIR
REC_CHUNK = 16
REC_FAST = 64
FAST_LIMIT = 60.0
VMEM_LIMIT = 56 * 1024 * 1024


def _params(sem, vmem=VMEM_LIMIT):
    return pltpu.CompilerParams(dimension_semantics=sem, vmem_limit_bytes=vmem)


def _const_spec(shape):
    nd = len(shape)
    return pl.BlockSpec(shape, lambda *_: (0,) * nd, pipeline_mode=pl.Buffered(1))


def _sigmoids(z):
    e = jnp.exp(-jnp.abs(z))
    r = 1.0 / (1.0 + e)
    er = e * r
    pos = z >= 0
    return jnp.where(pos, r, er), jnp.where(pos, er, r)


def _silu(z):
    return z * _sigmoids(z)[0]


def _rms(x, w):
    ms = jnp.mean(x * x, axis=-1, keepdims=True)
    return x * lax.rsqrt(ms + EPS) * w


def _mod_kernel(c_ref, w_ref, b_ref, o_ref):
    cond = _silu(c_ref[...])
    o_ref[0] = jnp.dot(cond, w_ref[0], preferred_element_type=F32) + b_ref[0]


def _modulation(c, w_ada, b_ada):
    depth, d, n = w_ada.shape
    bsz = c.shape[0]
    bp = -(-bsz // 8) * 8
    cp = jnp.zeros((bp, d), F32).at[:bsz].set(c)
    tn = 1536
    out = pl.pallas_call(
        _mod_kernel,
        grid=(depth, n // tn),
        in_specs=[pl.BlockSpec((bp, d), lambda l, j: (0, 0)),
                  pl.BlockSpec((1, d, tn), lambda l, j: (l, 0, j)),
                  pl.BlockSpec((1, 1, tn), lambda l, j: (l, 0, j))],
        out_specs=pl.BlockSpec((1, bp, tn), lambda l, j: (l, 0, j)),
        out_shape=jax.ShapeDtypeStruct((depth, bp, n), F32),
        compiler_params=_params(("arbitrary", "arbitrary")),
        name="adaln_mod",
    )(cp, w_ada, b_ada.reshape(depth, 1, n))
    return out[:, :bsz].reshape(depth, bsz, N_MOD, d)


def _inproj_kernel(x_ref, mod_ref, nw_ref, w_ref, lbg_ref,
                   a_ref, q_ref, k_ref, v_ref, q4_ref, k4_ref, v4_ref, q16_ref, k16_ref, v16_ref,
                   qr_ref, kf_ref, lf_ref, kb_ref, lb_ref, vr_ref, gr_ref, nat_ref, *, layer, tm):
    mod = mod_ref[0]
    h = (_rms(x_ref[...], nw_ref[...]) * (1.0 + mod[1:2, :]) + mod[0:1, :]).astype(BF16)

    def proj(c0, n):
        return jnp.dot(h, w_ref[:, c0:c0 + n], preferred_element_type=F32)

    ag = proj(0, 2 * CONV_CH)
    a_ref[...] = ag[:, :CONV_CH] * _sigmoids(ag[:, CONV_CH:])[0]
    c0 = 2 * CONV_CH
    qa = proj(c0, ATT_WIDTH) * (HEAD_DIM ** -0.5)
    ka = proj(c0 + ATT_WIDTH, ATT_WIDTH)
    va = proj(c0 + 2 * ATT_WIDTH, ATT_WIDTH)
    c0 += 3 * ATT_WIDTH
    views = ((q_ref, q4_ref, q16_ref), (k_ref, k4_ref, k16_ref), (v_ref, v4_ref, v16_ref))
    for t, (val, outs) in enumerate(zip((qa, ka, va), views)):
        outs[0][...] = val.astype(BF16)
        for g in range(ATT_WIDTH // LANES):
            nat_ref[t, g] = val[:, g * LANES:(g + 1) * LANES]
        for out_ref, dil in zip(outs[1:], DILATIONS[1:]):
            for res in range(dil):
                rows = pl.ds(res, tm // dil, stride=dil)
                piece = jnp.concatenate([nat_ref[t, g, rows, :] for g in range(ATT_WIDTH // LANES)], axis=-1)
                out_ref[:, res * ATT_WIDTH:(res + 1) * ATT_WIDTH] = piece.astype(BF16)

    g = lbg_ref[...]
    ge = jnp.exp(g - jnp.max(g, axis=0, keepdims=True))
    p = ge / jnp.sum(ge, axis=0, keepdims=True)
    lbv = jnp.zeros(p.shape[1:], F32)
    for j in range(1, layer + 1):
        lbv = lbv + p[j]

    zq = proj(c0, REC_WIDTH)
    zf = proj(c0 + REC_WIDTH, REC_WIDTH)
    zb = proj(c0 + 2 * REC_WIDTH, REC_WIDTH)
    zi = proj(c0 + 3 * REC_WIDTH, REC_WIDTH)
    zg = proj(c0 + 4 * REC_WIDTH, REC_WIDTH)

    def gate(z, lb):
        sp, sn = _sigmoids(z)
        f = lb + (1.0 - lb) * sp
        return jnp.log(jnp.maximum(f, F_TINY)), (1.0 - lb) * sn

    qs = _silu(zq)
    gs = _silu(zg)
    lff, kff = gate(zf, lbv[0:1, :])
    lbb, kbb = gate(zb, lbv[1:2, :])
    for pr in range(N_PAIRS):
        sl = slice(pr * PAIR, (pr + 1) * PAIR)
        qr_ref[pr] = qs[:, sl]
        gr_ref[pr] = gs[:, sl]
        vr_ref[pr] = zi[:, sl]
        lf_ref[pr] = lff[:, sl]
        kf_ref[pr] = kff[:, sl]
        lb_ref[pr] = lbb[:, sl]
        kb_ref[pr] = kbb[:, sl]


def _inproj(x2, mod_l, norm_w, w_in_bf, lb_gamma, layer, seq, tm):
    n, d = x2.shape
    tiles_per_seq = seq // tm
    row = lambda i: (i, 0)
    rec_spec = pl.BlockSpec((N_PAIRS, tm, PAIR), lambda i: (0, i, 0))
    rec_shape = jax.ShapeDtypeStruct((N_PAIRS, n, PAIR), F32)
    return pl.pallas_call(
        functools.partial(_inproj_kernel, layer=layer, tm=tm),
        grid=(n // tm,),
        in_specs=[pl.BlockSpec((tm, d), row),
                  pl.BlockSpec((1, N_MOD, d), lambda i: (i // tiles_per_seq, 0, 0)),
                  _const_spec((1, d)),
                  _const_spec((d, IN_COLS)),
                  _const_spec(lb_gamma.shape)],
        out_specs=[pl.BlockSpec((tm, CONV_CH), row)]
                  + [pl.BlockSpec((tm // dil, dil * ATT_WIDTH), row) for dil in DILATIONS for _ in range(3)]
                  + [rec_spec] * 7,
        out_shape=[jax.ShapeDtypeStruct((n, CONV_CH), F32)]
                  + [jax.ShapeDtypeStruct((n // dil, dil * ATT_WIDTH), BF16) for dil in DILATIONS for _ in range(3)]
                  + [rec_shape] * 7,
        scratch_shapes=[pltpu.VMEM((3, ATT_WIDTH // LANES, tm, LANES), F32)],
        compiler_params=_params(("arbitrary",)),
        name="inproj",
    )(x2, mod_l, norm_w.reshape(1, d), w_in_bf, lb_gamma)


CONV_HALO = 16
CONV_ROWS = 64


def _conva_kernel(a_ref, ap_ref, an_ref, w_ref, b_ref, lw_ref, lb_ref, o_ref, ext_ref, y_ref, *, seq, tm):
    i = pl.program_id(0)
    t0 = (i * tm) % seq
    groups = CONV_CH // LANES
    prev = jnp.where(t0 == 0, 0.0, ap_ref[...])
    nxt = jnp.where(t0 + tm == seq, 0.0, an_ref[...])
    for g in range(groups):
        cols = slice(g * LANES, (g + 1) * LANES)
        ext_ref[g, 0:CONV_HALO] = prev[:, cols]
        ext_ref[g, CONV_HALO:CONV_HALO + tm] = a_ref[:, cols]
        ext_ref[g, CONV_HALO + tm:] = nxt[:, cols]
    w = w_ref[...]
    shift = CONV_HALO - CONV_WIDTH // 2
    for r0 in range(0, tm, 2 * CONV_ROWS):
        for parity in range(2):
            acc = jnp.zeros((CONV_ROWS, CONV_CH), F32) + b_ref[...]
            for j in range(CONV_WIDTH):
                rows = pl.ds(r0 + parity + j + shift, CONV_ROWS, stride=2)
                tap = jnp.concatenate([ext_ref[g, rows, :] for g in range(groups)], axis=-1)
                acc = acc + w[j:j + 1, :] * tap
            mu = jnp.mean(acc, axis=-1, keepdims=True)
            cen = acc - mu
            var = jnp.mean(cen * cen, axis=-1, keepdims=True)
            y = _silu(cen * lax.rsqrt(var + EPS) * lw_ref[...] + lb_ref[...])
            for g in range(groups):
                y_ref[g, pl.ds(r0 + parity, CONV_ROWS, stride=2), :] = y[:, g * LANES:(g + 1) * LANES]
    o_ref[...] = jnp.concatenate([y_ref[g] for g in range(groups)], axis=-1).astype(BF16)


def _conv_a(a, conv_w, conv_b, ln_w, ln_b, seq, tm):
    n, ch = a.shape
    hb = tm // CONV_HALO
    nhb = n // CONV_HALO
    vec = lambda v: v.reshape(1, ch)
    return pl.pallas_call(
        functools.partial(_conva_kernel, seq=seq, tm=tm),
        grid=(n // tm,),
        in_specs=[pl.BlockSpec((tm, ch), lambda i: (i, 0)),
                  pl.BlockSpec((CONV_HALO, ch), lambda i: (jnp.maximum(i * hb - 1, 0), 0)),
                  pl.BlockSpec((CONV_HALO, ch), lambda i: (jnp.minimum((i + 1) * hb, nhb - 1), 0)),
                  _const_spec((CONV_WIDTH, ch)), _const_spec((1, ch)),
                  _const_spec((1, ch)), _const_spec((1, ch))],
        out_specs=pl.BlockSpec((tm, ch), lambda i: (i, 0)),
        out_shape=jax.ShapeDtypeStruct((n, ch), BF16),
        scratch_shapes=[pltpu.VMEM((ch // LANES, tm + 2 * CONV_HALO, LANES), F32),
                        pltpu.VMEM((ch // LANES, tm, LANES), F32)],
        compiler_params=_params(("arbitrary",)),
        name="conv_a",
    )(a, a, a, conv_w, vec(conv_b), vec(ln_w), vec(ln_b))


def _attn_kernel(q_ref, k_ref, kp_ref, kn_ref, v_ref, vp_ref, vn_ref, bias_ref,
                 o_ref, l_ref, kx_ref, vx_ref, *, tq, sub_len):
    i = pl.program_id(1)
    kx_ref[0:BAND_HALF] = kp_ref[...]
    kx_ref[BAND_HALF:BAND_HALF + tq] = k_ref[...]
    kx_ref[BAND_HALF + tq:] = kn_ref[...]
    vx_ref[0:BAND_HALF] = vp_ref[...]
    vx_ref[BAND_HALF:BAND_HALF + tq] = v_ref[...]
    vx_ref[BAND_HALF + tq:] = vn_ref[...]
    span = ATT_BLOCK + 2 * BAND_HALF
    first = lax.broadcasted_iota(jnp.int32, (ATT_BLOCK, PAIR), 1) < HEAD_DIM
    col = lax.broadcasted_iota(jnp.int32, (1, span), 1)
    nt = (((1,), (1,)), ((), ()))
    for blk in range(tq // ATT_BLOCK):
        r0 = blk * ATT_BLOCK
        kpos = i * tq + (r0 - BAND_HALF) + col
        valid = (kpos >= 0) & (kpos < sub_len)
        for pr in range(N_PAIRS):
            sl = slice(pr * PAIR, (pr + 1) * PAIR)
            qp = q_ref[r0:r0 + ATT_BLOCK, sl]
            kp = kx_ref[r0:r0 + span, sl]
            vp = vx_ref[r0:r0 + span, sl]
            res = []
            for hh in range(2):
                keep = first if hh == 0 else jnp.logical_not(first)
                qm = jnp.where(keep, qp, jnp.zeros_like(qp))
                s = lax.dot_general(qm, kp, nt, preferred_element_type=F32)
                s = jnp.where(valid, s + bias_ref[2 * pr + hh], MASK_VALUE)
                m = jnp.max(s, axis=-1, keepdims=True)
                pe = jnp.exp(s - m)
                den = jnp.sum(pe, axis=-1, keepdims=True)
                o = jnp.dot(pe.astype(BF16), vp, preferred_element_type=F32)
                res.append((o / den, m + jnp.log(den)))
            o_ref[r0:r0 + ATT_BLOCK, sl] = jnp.where(first, res[0][0], res[1][0])
            l_ref[r0:r0 + ATT_BLOCK, sl] = jnp.where(first, res[0][1], res[1][1])


def _attn_bias(dilation):
    slopes = 2.0 ** (-ALIBI_MAX_EXP * np.arange(1, N_ATT_HEADS + 1) / N_ATT_HEADS)
    span = ATT_BLOCK + 2 * BAND_HALF
    rel = np.arange(span)[None, :] - BAND_HALF - np.arange(ATT_BLOCK)[:, None]
    dist = (np.abs(rel) * dilation).astype(np.float32)
    bias = -(slopes.astype(np.float32)[:, None, None] * dist[None])
    bias = np.where((np.abs(rel) <= BAND_HALF)[None], bias, np.float32(MASK_VALUE))
    return jnp.asarray(bias, F32)


def _attn_branch(q, k, v, bsz, seq, dilation):
    sub_len = seq // dilation
    tq = min(sub_len, 1024)
    nq = sub_len // tq
    hb = tq // BAND_HALF
    nhb = sub_len // BAND_HALF
    d = dilation
    main = pl.BlockSpec((tq, ATT_WIDTH), lambda b, i: ((b // d) * nq + i, b % d))
    prev = pl.BlockSpec((BAND_HALF, ATT_WIDTH),
                        lambda b, i: ((b // d) * nhb + jnp.maximum(i * hb - 1, 0), b % d))
    nxt = pl.BlockSpec((BAND_HALF, ATT_WIDTH),
                       lambda b, i: ((b // d) * nhb + jnp.minimum((i + 1) * hb, nhb - 1), b % d))
    span = ATT_BLOCK + 2 * BAND_HALF
    return pl.pallas_call(
        functools.partial(_attn_kernel, tq=tq, sub_len=sub_len),
        grid=(bsz * d, nq),
        in_specs=[main, main, prev, nxt, main, prev, nxt,
                  _const_spec((N_ATT_HEADS, ATT_BLOCK, span))],
        out_specs=[main, main],
        out_shape=[jax.ShapeDtypeStruct(q.shape, F32)] * 2,
        scratch_shapes=[pltpu.VMEM((tq + 2 * BAND_HALF, ATT_WIDTH), BF16)] * 2,
        compiler_params=_params(("arbitrary", "arbitrary")),
        name=f"attn_d{dilation}",
    )(q, k, k, k, v, v, v, _attn_bias(dilation))


def _cumsum_rows(tri, x):
    hi = x.astype(BF16)
    r1 = x - hi.astype(F32)
    mid = r1.astype(BF16)
    lo = (r1 - mid.astype(F32)).astype(BF16)
    dot = lambda t: jnp.dot(tri, t, preferred_element_type=F32)
    return dot(hi) + dot(mid) + dot(lo)


def _rec_chunk(q_ref, k_ref, v_ref, b_ref, st_ref, o_ref, ones_ref, bd_ref, pr, r, reverse):
    c = REC_CHUNK
    rows = pl.ds(r, c)
    q = q_ref[pr, rows, :]
    k = k_ref[pr, rows, :]
    v = v_ref[pr, rows, :]
    b = b_ref[pr, rows, :]
    edge = b[0:1, :] if reverse else b[c - 1:c, :]
    q_in = (q * jnp.exp(b)).astype(BF16)
    k_st = (k * jnp.exp(edge - b)).astype(BF16)
    st = st_ref[pr]
    o_inter = lax.dot_general(q_in, st.astype(BF16), (((1,), (1,)), ((), ())),
                              preferred_element_type=F32)
    upd = lax.dot_general(v.astype(BF16), k_st, (((0,), (0,)), ((), ())),
                          preferred_element_type=F32)
    st_ref[pr] = st * jnp.exp(edge) + upd * bd_ref[...]
    t_idx = lax.broadcasted_iota(jnp.int32, (c, PAIR), 0)
    terms = []
    for s in range(c):
        ok = (t_idx <= s) if reverse else (t_idx >= s)
        e = jnp.exp(jnp.where(ok, b - b[s:s + 1, :], MASK_VALUE))
        terms.append(q * e * k[s:s + 1, :])
    a = jnp.concatenate(terms, axis=0).astype(BF16)
    sc = jnp.dot(a, ones_ref[...], preferred_element_type=F32)
    o = o_inter
    for s in range(c):
        o = o + sc[s * c:(s + 1) * c, :] * v[s:s + 1, :]
    o_ref[pr, rows, :] = o


def _rec_fast_chunk(q_ref, k_ref, v_ref, b_ref, st_ref, o_ref, bd_ref, tri, pr, r0, reverse):
    c = REC_FAST
    rows = slice(r0, r0 + c)
    q = q_ref[pr, rows, :]
    k = k_ref[pr, rows, :]
    vb = v_ref[pr, rows, :].astype(BF16)
    b = b_ref[pr, rows, :]
    mid = b[c // 2:c // 2 + 1, :]
    edge = b[0:1, :] if reverse else b[c - 1:c, :]
    qt = q * jnp.exp(b - mid)
    kt = k * jnp.exp(mid - b)
    q_in = (qt * jnp.exp(mid)).astype(BF16)
    k_st = (kt * jnp.exp(edge - mid)).astype(BF16)
    ktb = kt.astype(BF16)
    first = lax.broadcasted_iota(jnp.int32, (c, PAIR), 1) < HEAD_DIM
    zero = jnp.zeros_like(ktb)
    k2 = jnp.concatenate([jnp.where(first, ktb, zero), jnp.where(first, zero, ktb)], axis=0)
    v2 = jnp.concatenate([jnp.where(first, vb, zero), jnp.where(first, zero, vb)], axis=0)
    nt = (((1,), (1,)), ((), ()))
    s = lax.dot_general(qt.astype(BF16), k2, nt, preferred_element_type=F32)
    p = jnp.where(tri, s, 0.0).astype(BF16)
    st = st_ref[pr]
    o = (jnp.dot(p, v2, preferred_element_type=F32)
         + lax.dot_general(q_in, st.astype(BF16), nt, preferred_element_type=F32))
    upd = lax.dot_general(vb, k_st, (((0,), (0,)), ((), ())), preferred_element_type=F32)
    st_ref[pr] = st * jnp.exp(edge) + upd * bd_ref[...]
    o_ref[pr, rows, :] = o


def _hgrn_kernel(qf_ref, kf_ref, lf_ref, vf_ref, qb_ref, kb_ref, lb_ref, vb_ref,
                 tril_ref, triu_ref, fl_ref, fu_ref, ones_ref, bd_ref,
                 of_ref, ob_ref, stf_ref, stb_ref, bf_ref, bb_ref, *, tt):
    @pl.when(pl.program_id(1) == 0)
    def _():
        stf_ref[...] = jnp.zeros_like(stf_ref)
        stb_ref[...] = jnp.zeros_like(stb_ref)

    c = REC_FAST
    dev = jnp.zeros((c, PAIR), F32)
    for r0 in range(0, tt, c):
        for pr in range(N_PAIRS):
            for l_ref, t_ref, b_ref in ((lf_ref, fl_ref, bf_ref), (lb_ref, fu_ref, bb_ref)):
                b = _cumsum_rows(t_ref[...], l_ref[pr, r0:r0 + c, :])
                b_ref[pr, r0:r0 + c, :] = b
                dev = jnp.maximum(dev, jnp.abs(b - b[c // 2:c // 2 + 1, :]))
    safe = jnp.max(dev) <= FAST_LIMIT

    @pl.when(safe)
    def _():
        row = lax.broadcasted_iota(jnp.int32, (c, PAIR), 0)
        col = lax.broadcasted_iota(jnp.int32, (c, PAIR), 1) % c
        lower, upper = row >= col, row <= col
        for r0 in range(0, tt, c):
            for pr in range(N_PAIRS):
                _rec_fast_chunk(qf_ref, kf_ref, vf_ref, bf_ref, stf_ref, of_ref, bd_ref, lower, pr, r0, False)
                _rec_fast_chunk(qb_ref, kb_ref, vb_ref, bb_ref, stb_ref, ob_ref, bd_ref, upper, pr,
                                tt - c - r0, True)

    @pl.when(jnp.logical_not(safe))
    def _():
        for pr in range(N_PAIRS):
            bf_ref[pr] = _cumsum_rows(tril_ref[...], lf_ref[pr])
            bb_ref[pr] = _cumsum_rows(triu_ref[...], lb_ref[pr])

        def body(ci, carry):
            rf = pl.multiple_of(ci * REC_CHUNK, REC_CHUNK)
            rb = pl.multiple_of(tt - REC_CHUNK - ci * REC_CHUNK, REC_CHUNK)
            for pr in range(N_PAIRS):
                _rec_chunk(qf_ref, kf_ref, vf_ref, bf_ref, stf_ref, of_ref, ones_ref, bd_ref, pr, rf, False)
                _rec_chunk(qb_ref, kb_ref, vb_ref, bb_ref, stb_ref, ob_ref, ones_ref, bd_ref, pr, rb, True)
            return carry

        lax.fori_loop(0, tt // REC_CHUNK, body, 0)


def _tri_pair(size, block):
    idx = np.arange(size)
    same = (idx[:, None] // block) == (idx[None, :] // block)
    lower = same & (idx[:, None] >= idx[None, :])
    upper = same & (idx[:, None] <= idx[None, :])
    return jnp.asarray(lower, BF16), jnp.asarray(upper, BF16)


def _hgrn(qr, kf, lf, kb, lb, vr, bsz, seq, tt):
    n = qr.shape[1]
    nt = seq // tt
    fwd = pl.BlockSpec((N_PAIRS, tt, PAIR), lambda b, i: (0, b * nt + i, 0))
    bwd = pl.BlockSpec((N_PAIRS, tt, PAIR), lambda b, i: (0, b * nt + nt - 1 - i, 0))
    tril, triu = _tri_pair(tt, REC_CHUNK)
    fl, fu = _tri_pair(REC_FAST, REC_FAST)
    head = np.arange(PAIR) // HEAD_DIM
    bd = head[:, None] == head[None, :]
    shape = jax.ShapeDtypeStruct((N_PAIRS, n, PAIR), F32)
    return pl.pallas_call(
        functools.partial(_hgrn_kernel, tt=tt),
        grid=(bsz, nt),
        in_specs=[fwd, fwd, fwd, fwd, bwd, bwd, bwd, bwd,
                  _const_spec((tt, tt)), _const_spec((tt, tt)),
                  _const_spec((REC_FAST, REC_FAST)), _const_spec((REC_FAST, REC_FAST)),
                  _const_spec((PAIR, PAIR)), _const_spec((PAIR, PAIR))],
        out_specs=[fwd, bwd],
        out_shape=[shape, shape],
        scratch_shapes=[pltpu.VMEM((N_PAIRS, PAIR, PAIR), F32)] * 2
                       + [pltpu.VMEM((N_PAIRS, tt, PAIR), F32)] * 2,
        compiler_params=_params(("arbitrary", "arbitrary")),
        name="hgrn2",
    )(qr, kf, lf, vr, qr, kb, lb, vr, tril, triu, fl, fu, jnp.asarray(bd, BF16), jnp.asarray(bd, F32))


def _outproj_kernel(x_ref, mod_ref, a_ref, o1_ref, l1_ref, o4_ref, l4_ref, o16_ref, l16_ref,
                    of_ref, ob_ref, gr_ref, rw_ref, w_ref, y_ref, nat_ref, *, tm):
    acc = jnp.dot(a_ref[...], w_ref[0:CONV_CH, :], preferred_element_type=F32)

    groups = ATT_WIDTH // LANES

    def natural(slot, ref, dil):
        for res in range(dil):
            for g in range(groups):
                c0 = res * ATT_WIDTH + g * LANES
                nat_ref[slot, g, pl.ds(res, tm // dil, stride=dil), :] = ref[:, c0:c0 + LANES]
        return jnp.concatenate([nat_ref[slot, g] for g in range(groups)], axis=-1)

    o1, l1 = o1_ref[...], l1_ref[...]
    o2, l2 = natural(0, o4_ref, DILATIONS[1]), natural(1, l4_ref, DILATIONS[1])
    o3, l3 = natural(2, o16_ref, DILATIONS[2]), natural(3, l16_ref, DILATIONS[2])
    m = jnp.maximum(jnp.maximum(l1, l2), l3)
    e1, e2, e3 = jnp.exp(l1 - m), jnp.exp(l2 - m), jnp.exp(l3 - m)
    att = (e1 * o1 + e2 * o2 + e3 * o3) / (e1 + e2 + e3)
    acc += jnp.dot(att.astype(BF16), w_ref[CONV_CH:CONV_CH + ATT_WIDTH, :], preferred_element_type=F32)

    first = lax.broadcasted_iota(jnp.int32, (1, PAIR), 1) < HEAD_DIM
    base = CONV_CH + ATT_WIDTH
    for pr in range(N_PAIRS):
        o = of_ref[pr] + ob_ref[pr]
        sq = o * o
        s0 = jnp.sum(jnp.where(first, sq, 0.0), axis=-1, keepdims=True)
        s1 = jnp.sum(jnp.where(first, 0.0, sq), axis=-1, keepdims=True)
        ms = jnp.where(first, s0, s1) * (1.0 / HEAD_DIM)
        rec = o * lax.rsqrt(ms + EPS) * rw_ref[:, pr * PAIR:(pr + 1) * PAIR] * gr_ref[pr]
        acc += jnp.dot(rec.astype(BF16), w_ref[base + pr * PAIR:base + (pr + 1) * PAIR, :],
                       preferred_element_type=F32)
    y_ref[...] = x_ref[...] + mod_ref[0][2:3, :] * acc


def _outproj(x2, mod_l, a, branches, o_f, o_b, gr, rec_norm_w, w_out_bf, seq, tm):
    n, d = x2.shape
    tiles_per_seq = seq // tm
    row = lambda i: (i, 0)
    rec = pl.BlockSpec((N_PAIRS, tm, PAIR), lambda i: (0, i, 0))
    att = [pl.BlockSpec((tm // dil, dil * ATT_WIDTH), row) for dil in DILATIONS for _ in range(2)]
    return pl.pallas_call(
        functools.partial(_outproj_kernel, tm=tm),
        grid=(n // tm,),
        in_specs=[pl.BlockSpec((tm, d), row),
                  pl.BlockSpec((1, N_MOD, d), lambda i: (i // tiles_per_seq, 0, 0)),
                  pl.BlockSpec((tm, CONV_CH), row)] + att + [rec] * 3
                 + [_const_spec((1, REC_WIDTH)), _const_spec((d, d))],
        out_specs=pl.BlockSpec((tm, d), row),
        out_shape=jax.ShapeDtypeStruct((n, d), F32),
        scratch_shapes=[pltpu.VMEM((4, ATT_WIDTH // LANES, tm, LANES), F32)],
        compiler_params=_params(("arbitrary",)),
        name="outproj",
    )(x2, mod_l, a, *[t for pair in branches for t in pair], o_f, o_b, gr,
      rec_norm_w.reshape(1, REC_WIDTH), w_out_bf)


FFN_HALO = 16
FFN_TILE = 256


def _ffn_kernel(x_ref, xp_ref, xn_ref, mod_ref, nw_ref, wup_ref, cw_ref, wd_ref, fw_ref,
                y_ref, hx_ref, acc_ref, u_ref, yi_ref, *, seq, tm, final):
    i = pl.program_id(0)
    t0 = (i * tm) % seq
    mod = mod_ref[0]

    def pre(xv):
        return (_rms(xv, nw_ref[...]) * (1.0 + mod[4:5, :]) + mod[3:4, :]).astype(BF16)

    zero = jnp.zeros((FFN_HALO, D_MODEL), BF16)
    hx_ref[0:FFN_HALO] = jnp.where(t0 == 0, zero, pre(xp_ref[...]))
    hx_ref[FFN_HALO:FFN_HALO + tm] = pre(x_ref[...])
    hx_ref[FFN_HALO + tm:] = jnp.where(t0 + tm == seq, zero, pre(xn_ref[...]))

    n_steps = D_FF // FFN_TILE
    half = tm // 2

    groups = FFN_TILE // LANES

    def up(c):
        hx = hx_ref[...]
        for part, base in enumerate((0, D_FF)):
            cols = slice(base + c * FFN_TILE, base + (c + 1) * FFN_TILE)
            u = jnp.dot(hx, wup_ref[:, cols], preferred_element_type=F32)
            for g in range(groups):
                u_ref[c % 2, part, g] = u[:, g * LANES:(g + 1) * LANES]

    def taps(c, part, parity):
        base = part * D_FF + c * FFN_TILE
        w = cw_ref[:, base:base + FFN_TILE]

        def row(off):
            rows = pl.ds(FFN_HALO + parity + off, half, stride=2)
            return jnp.concatenate([u_ref[c % 2, part, g, rows, :] for g in range(groups)], axis=-1)

        return w[0:1, :] * row(-1) + w[1:2, :] * row(0) + w[2:3, :] * row(1)

    up(0)
    for c in range(n_steps):
        if c + 1 < n_steps:
            up(c + 1)
        for parity in range(2):
            gate = taps(c, 0, parity)
            val = taps(c, 1, parity)
            act = (0.5 * gate * (1.0 + lax.erf(gate * (2.0 ** -0.5))) * val).astype(BF16)
            part = jnp.dot(act, wd_ref[c * FFN_TILE:(c + 1) * FFN_TILE, :], preferred_element_type=F32)
            if c == 0:
                acc_ref[parity] = part
            else:
                acc_ref[parity] += part
    for parity in range(2):
        for g in range(D_MODEL // LANES):
            yi_ref[g, pl.ds(parity, half, stride=2), :] = acc_ref[parity, :, g * LANES:(g + 1) * LANES]
    ffn = jnp.concatenate([yi_ref[g] for g in range(D_MODEL // LANES)], axis=-1)
    y = x_ref[...] + mod[5:6, :] * ffn
    if final:
        y = _rms(y, fw_ref[...])
    y_ref[...] = y


def _ffn(x2, mod_l, norm_w, w_up_bf, conv_w, w_down_bf, final_w, final, seq, tm):
    n, d = x2.shape
    tiles_per_seq = seq // tm
    hb = tm // FFN_HALO
    nhb = n // FFN_HALO
    return pl.pallas_call(
        functools.partial(_ffn_kernel, seq=seq, tm=tm, final=final),
        grid=(n // tm,),
        in_specs=[pl.BlockSpec((tm, d), lambda i: (i, 0)),
                  pl.BlockSpec((FFN_HALO, d), lambda i: (jnp.maximum(i * hb - 1, 0), 0)),
                  pl.BlockSpec((FFN_HALO, d), lambda i: (jnp.minimum((i + 1) * hb, nhb - 1), 0)),
                  pl.BlockSpec((1, N_MOD, d), lambda i: (i // tiles_per_seq, 0, 0)),
                  _const_spec((1, d)), _const_spec((d, 2 * D_FF)), _const_spec((3, 2 * D_FF)),
                  _const_spec((D_FF, d)), _const_spec((1, d))],
        out_specs=pl.BlockSpec((tm, d), lambda i: (i, 0)),
        out_shape=jax.ShapeDtypeStruct((n, d), F32),
        scratch_shapes=[pltpu.VMEM((tm + 2 * FFN_HALO, d), BF16), pltpu.VMEM((2, tm // 2, d), F32),
                        pltpu.VMEM((2, 2, FFN_TILE // LANES, tm + 2 * FFN_HALO, LANES), F32),
                        pltpu.VMEM((d // LANES, tm, LANES), F32)],
        compiler_params=_params(("arbitrary",)),
        name="conv_ffn",
    )(x2, x2, x2, mod_l, norm_w.reshape(1, d), w_up_bf, conv_w, w_down_bf, final_w.reshape(1, d))


def kernel(x, c, w_ada, b_ada, norm1_w, w_in, conv_a_w, conv_a_b, ln_a_w, ln_a_b, lb_gamma,
           rec_norm_w, w_out, norm2_w, w_up, conv_f_w, w_down, final_norm_w):
    bsz, seq, d = x.shape
    depth = w_in.shape[0]
    n = bsz * seq
    tm = min(512, seq)
    tm_ffn = min(1024, seq)
    tt = min(256, seq)
    mod = _modulation(c, w_ada, b_ada)
    x2 = x.reshape(n, d)
    for l in range(depth):
        outs = _inproj(x2, mod[l], norm1_w[l], w_in[l].astype(BF16), lb_gamma, l, seq, tm)
        a, qkv, (qr, kf, lf, kb, lb, vr, gr) = outs[0], outs[1:10], outs[10:]
        a = _conv_a(a, conv_a_w[l], conv_a_b[l], ln_a_w[l], ln_a_b[l], seq, tm)
        branches = [_attn_branch(*qkv[3 * g:3 * g + 3], bsz, seq, dil) for g, dil in enumerate(DILATIONS)]
        o_f, o_b = _hgrn(qr, kf, lf, kb, lb, vr, bsz, seq, tt)
        x2 = _outproj(x2, mod[l], a, branches, o_f, o_b, gr, rec_norm_w[l], w_out[l].astype(BF16), seq, tm)
        x2 = _ffn(x2, mod[l], norm2_w[l], w_up[l].astype(BF16), conv_f_w[l], w_down[l].astype(BF16),
                  final_norm_w, l == depth - 1, seq, tm_ffn)
    return x2.reshape(bsz, seq, d)
```

```python
import functools

import numpy as np
import jax
import jax.numpy as jnp
from jax import lax
from jax.experimental import pallas as pl
from jax.experimental.pallas import tpu as pltpu

F32 = jnp.float32
BF16 = jnp.bfloat16

D_MODEL = 1024
HEAD_DIM = 64
CONV_CH = 256
CONV_WIDTH = 31
ATT_WIDTH = 384
N_ATT_HEADS = 6
DILATIONS = (1, 4, 16)
BAND_HALF = 64
ATT_BLOCK = 128
ALIBI_MAX_EXP = 8.0
MASK_VALUE = -1e30
REC_WIDTH = 384
F_TINY = 1e-30
D_FF = 2816
N_MOD = 6
EPS = 1e-6
IN_COLS = 3584

LANES = 128
PAIR = 2 * HEAD_DIM
N_PAIRS = REC_WIDTH // PAIR
REC_CHUNK = 16
REC_FAST = 64
FAST_LIMIT = 60.0
VMEM_LIMIT = 56 * 1024 * 1024


def _params(sem, vmem=VMEM_LIMIT):
    return pltpu.CompilerParams(dimension_semantics=sem, vmem_limit_bytes=vmem)


def _const_spec(shape):
    nd = len(shape)
    return pl.BlockSpec(shape, lambda *_: (0,) * nd, pipeline_mode=pl.Buffered(1))


def _sigmoids(z):
    e = jnp.exp(-jnp.abs(z))
    r = 1.0 / (1.0 + e)
    er = e * r
    pos = z >= 0
    return jnp.where(pos, r, er), jnp.where(pos, er, r)


def _silu(z):
    return z * _sigmoids(z)[0]


def _rms(x, w):
    ms = jnp.mean(x * x, axis=-1, keepdims=True)
    return x * lax.rsqrt(ms + EPS) * w


def _mod_kernel(c_ref, w_ref, b_ref, o_ref):
    cond = _silu(c_ref[...])
    o_ref[0] = jnp.dot(cond, w_ref[0], preferred_element_type=F32) + b_ref[0]


def _modulation(c, w_ada, b_ada):
    depth, d, n = w_ada.shape
    bsz = c.shape[0]
    bp = -(-bsz // 8) * 8
    cp = jnp.zeros((bp, d), F32).at[:bsz].set(c)
    tn = 1536
    out = pl.pallas_call(
        _mod_kernel,
        grid=(depth, n // tn),
        in_specs=[pl.BlockSpec((bp, d), lambda l, j: (0, 0)),
                  pl.BlockSpec((1, d, tn), lambda l, j: (l, 0, j)),
                  pl.BlockSpec((1, 1, tn), lambda l, j: (l, 0, j))],
        out_specs=pl.BlockSpec((1, bp, tn), lambda l, j: (l, 0, j)),
        out_shape=jax.ShapeDtypeStruct((depth, bp, n), F32),
        compiler_params=_params(("arbitrary", "arbitrary")),
        name="adaln_mod",
    )(cp, w_ada, b_ada.reshape(depth, 1, n))
    return out[:, :bsz].reshape(depth, bsz, N_MOD, d)


def _inproj_kernel(x_ref, mod_ref, nw_ref, w_ref, lbg_ref,
                   a_ref, q_ref, k_ref, v_ref, q4_ref, k4_ref, v4_ref, q16_ref, k16_ref, v16_ref,
                   qr_ref, kf_ref, lf_ref, kb_ref, lb_ref, vr_ref, gr_ref, nat_ref, *, layer, tm):
    mod = mod_ref[0]
    h = (_rms(x_ref[...], nw_ref[...]) * (1.0 + mod[1:2, :]) + mod[0:1, :]).astype(BF16)

    def proj(c0, n):
        return jnp.dot(h, w_ref[:, c0:c0 + n], preferred_element_type=F32)

    assert ATT_WIDTH == REC_WIDTH
    w2 = 2 * ATT_WIDTH
    ag = proj(0, 2 * CONV_CH)
    a_ref[...] = ag[:, :CONV_CH] * _sigmoids(ag[:, CONV_CH:])[0]
    c0 = 2 * CONV_CH
    qk = proj(c0, w2)
    vq = proj(c0 + w2, w2)
    qa = qk[:, :ATT_WIDTH] * (HEAD_DIM ** -0.5)
    ka = qk[:, ATT_WIDTH:]
    va = vq[:, :ATT_WIDTH]
    zq = vq[:, ATT_WIDTH:]
    c0 += 2 * w2
    views = ((q_ref, q4_ref, q16_ref), (k_ref, k4_ref, k16_ref), (v_ref, v4_ref, v16_ref))
    for t, (val, outs) in enumerate(zip((qa, ka, va), views)):
        outs[0][...] = val.astype(BF16)
        for g in range(ATT_WIDTH // LANES):
            nat_ref[t, g] = val[:, g * LANES:(g + 1) * LANES]
        for out_ref, dil in zip(outs[1:], DILATIONS[1:]):
            for res in range(dil):
                rows = pl.ds(res, tm // dil, stride=dil)
                piece = jnp.concatenate([nat_ref[t, g, rows, :] for g in range(ATT_WIDTH // LANES)], axis=-1)
                out_ref[:, res * ATT_WIDTH:(res + 1) * ATT_WIDTH] = piece.astype(BF16)

    g = lbg_ref[...]
    ge = jnp.exp(g - jnp.max(g, axis=0, keepdims=True))
    p = ge / jnp.sum(ge, axis=0, keepdims=True)
    lbv = jnp.zeros(p.shape[1:], F32)
    for j in range(1, layer + 1):
        lbv = lbv + p[j]

    zfb = proj(c0, w2)
    zig = proj(c0 + w2, w2)
    zf, zb = zfb[:, :REC_WIDTH], zfb[:, REC_WIDTH:]
    zi, zg = zig[:, :REC_WIDTH], zig[:, REC_WIDTH:]

    def gate(z, lb):
        sp, sn = _sigmoids(z)
        f = lb + (1.0 - lb) * sp
        return jnp.log(jnp.maximum(f, F_TINY)), (1.0 - lb) * sn

    qs = _silu(zq)
    gs = _silu(zg)
    lff, kff = gate(zf, lbv[0:1, :])
    lbb, kbb = gate(zb, lbv[1:2, :])
    for pr in range(N_PAIRS):
        sl = slice(pr * PAIR, (pr + 1) * PAIR)
        qr_ref[pr] = qs[:, sl].astype(BF16)
        gr_ref[pr] = gs[:, sl].astype(BF16)
        vr_ref[pr] = zi[:, sl].astype(BF16)
        lf_ref[pr] = lff[:, sl]
        kf_ref[pr] = kff[:, sl].astype(BF16)
        lb_ref[pr] = lbb[:, sl]
        kb_ref[pr] = kbb[:, sl].astype(BF16)


def _inproj(x2, mod_l, norm_w, w_in_bf, lb_gamma, layer, seq, tm):
    n, d = x2.shape
    tiles_per_seq = seq // tm
    row = lambda i: (i, 0)
    rec_spec = pl.BlockSpec((N_PAIRS, tm, PAIR), lambda i: (0, i, 0))
    rec_shapes = [jax.ShapeDtypeStruct((N_PAIRS, n, PAIR), dt) for dt in (BF16, BF16, F32, BF16, F32, BF16, BF16)]
    return pl.pallas_call(
        functools.partial(_inproj_kernel, layer=layer, tm=tm),
        grid=(n // tm,),
        in_specs=[pl.BlockSpec((tm, d), row),
                  pl.BlockSpec((1, N_MOD, d), lambda i: (i // tiles_per_seq, 0, 0)),
                  _const_spec((1, d)),
                  _const_spec((d, IN_COLS)),
                  _const_spec(lb_gamma.shape)],
        out_specs=[pl.BlockSpec((tm, CONV_CH), row)]
                  + [pl.BlockSpec((tm // dil, dil * ATT_WIDTH), row) for dil in DILATIONS for _ in range(3)]
                  + [rec_spec] * 7,
        out_shape=[jax.ShapeDtypeStruct((n, CONV_CH), F32)]
                  + [jax.ShapeDtypeStruct((n // dil, dil * ATT_WIDTH), BF16) for dil in DILATIONS for _ in range(3)]
                  + rec_shapes,
        scratch_shapes=[pltpu.VMEM((3, ATT_WIDTH // LANES, tm, LANES), F32)],
        compiler_params=_params(("arbitrary",)),
        name="inproj",
    )(x2, mod_l, norm_w.reshape(1, d), w_in_bf, lb_gamma)


CONV_HALO = 16
CONV_ROWS = 64


def _conva_kernel(a_ref, ap_ref, an_ref, w_ref, b_ref, lw_ref, lb_ref, o_ref, ext_ref, y_ref, *, seq, tm):
    i = pl.program_id(0)
    t0 = (i * tm) % seq
    groups = CONV_CH // LANES
    prev = jnp.where(t0 == 0, 0.0, ap_ref[...])
    nxt = jnp.where(t0 + tm == seq, 0.0, an_ref[...])
    for g in range(groups):
        cols = slice(g * LANES, (g + 1) * LANES)
        ext_ref[g, 0:CONV_HALO] = prev[:, cols]
        ext_ref[g, CONV_HALO:CONV_HALO + tm] = a_ref[:, cols]
        ext_ref[g, CONV_HALO + tm:] = nxt[:, cols]
    w = w_ref[...]
    shift = CONV_HALO - CONV_WIDTH // 2
    for r0 in range(0, tm, 2 * CONV_ROWS):
        for parity in range(2):
            acc = jnp.zeros((CONV_ROWS, CONV_CH), F32) + b_ref[...]
            for j in range(CONV_WIDTH):
                rows = pl.ds(r0 + parity + j + shift, CONV_ROWS, stride=2)
                tap = jnp.concatenate([ext_ref[g, rows, :] for g in range(groups)], axis=-1)
                acc = acc + w[j:j + 1, :] * tap
            mu = jnp.mean(acc, axis=-1, keepdims=True)
            cen = acc - mu
            var = jnp.mean(cen * cen, axis=-1, keepdims=True)
            y = _silu(cen * lax.rsqrt(var + EPS) * lw_ref[...] + lb_ref[...])
            for g in range(groups):
                y_ref[g, pl.ds(r0 + parity, CONV_ROWS, stride=2), :] = y[:, g * LANES:(g + 1) * LANES]
    o_ref[...] = jnp.concatenate([y_ref[g] for g in range(groups)], axis=-1).astype(BF16)


def _conv_a(a, conv_w, conv_b, ln_w, ln_b, seq, tm):
    n, ch = a.shape
    hb = tm // CONV_HALO
    nhb = n // CONV_HALO
    vec = lambda v: v.reshape(1, ch)
    return pl.pallas_call(
        functools.partial(_conva_kernel, seq=seq, tm=tm),
        grid=(n // tm,),
        in_specs=[pl.BlockSpec((tm, ch), lambda i: (i, 0)),
                  pl.BlockSpec((CONV_HALO, ch), lambda i: (jnp.maximum(i * hb - 1, 0), 0)),
                  pl.BlockSpec((CONV_HALO, ch), lambda i: (jnp.minimum((i + 1) * hb, nhb - 1), 0)),
                  _const_spec((CONV_WIDTH, ch)), _const_spec((1, ch)),
                  _const_spec((1, ch)), _const_spec((1, ch))],
        out_specs=pl.BlockSpec((tm, ch), lambda i: (i, 0)),
        out_shape=jax.ShapeDtypeStruct((n, ch), BF16),
        scratch_shapes=[pltpu.VMEM((ch // LANES, tm + 2 * CONV_HALO, LANES), F32),
                        pltpu.VMEM((ch // LANES, tm, LANES), F32)],
        compiler_params=_params(("arbitrary",)),
        name="conv_a",
    )(a, a, a, conv_w, vec(conv_b), vec(ln_w), vec(ln_b))


def _attn_kernel(q_ref, k_ref, kp_ref, kn_ref, v_ref, vp_ref, vn_ref, bias_ref,
                 o_ref, l_ref, kx_ref, vx_ref, *, tq, sub_len):
    i = pl.program_id(1)
    kx_ref[0:BAND_HALF] = kp_ref[...]
    kx_ref[BAND_HALF:BAND_HALF + tq] = k_ref[...]
    kx_ref[BAND_HALF + tq:] = kn_ref[...]
    vx_ref[0:BAND_HALF] = vp_ref[...]
    vx_ref[BAND_HALF:BAND_HALF + tq] = v_ref[...]
    vx_ref[BAND_HALF + tq:] = vn_ref[...]
    span = ATT_BLOCK + 2 * BAND_HALF
    first = lax.broadcasted_iota(jnp.int32, (ATT_BLOCK, PAIR), 1) < HEAD_DIM
    col = lax.broadcasted_iota(jnp.int32, (1, span), 1)
    nt = (((1,), (1,)), ((), ()))
    for blk in range(tq // ATT_BLOCK):
        r0 = blk * ATT_BLOCK
        kpos = i * tq + (r0 - BAND_HALF) + col
        valid = (kpos >= 0) & (kpos < sub_len)
        for pr in range(N_PAIRS):
            sl = slice(pr * PAIR, (pr + 1) * PAIR)
            qp = q_ref[r0:r0 + ATT_BLOCK, sl]
            kp = kx_ref[r0:r0 + span, sl]
            vp = vx_ref[r0:r0 + span, sl]
            res = []
            for hh in range(2):
                keep = first if hh == 0 else jnp.logical_not(first)
                qm = jnp.where(keep, qp, jnp.zeros_like(qp))
                s = lax.dot_general(qm, kp, nt, preferred_element_type=F32)
                s = jnp.where(valid, s + bias_ref[2 * pr + hh], MASK_VALUE)
                m = jnp.max(s, axis=-1, keepdims=True)
                pe = jnp.exp(s - m)
                den = jnp.sum(pe, axis=-1, keepdims=True)
                o = jnp.dot(pe.astype(BF16), vp, preferred_element_type=F32)
                res.append((o / den, m + jnp.log(den)))
            o_ref[r0:r0 + ATT_BLOCK, sl] = jnp.where(first, res[0][0], res[1][0]).astype(BF16)
            l_ref[r0:r0 + ATT_BLOCK, sl] = jnp.where(first, res[0][1], res[1][1])


def _attn_bias(dilation):
    slopes = 2.0 ** (-ALIBI_MAX_EXP * np.arange(1, N_ATT_HEADS + 1) / N_ATT_HEADS)
    span = ATT_BLOCK + 2 * BAND_HALF
    rel = np.arange(span)[None, :] - BAND_HALF - np.arange(ATT_BLOCK)[:, None]
    dist = (np.abs(rel) * dilation).astype(np.float32)
    bias = -(slopes.astype(np.float32)[:, None, None] * dist[None])
    bias = np.where((np.abs(rel) <= BAND_HALF)[None], bias, np.float32(MASK_VALUE))
    return jnp.asarray(bias, F32)


def _attn_branch(q, k, v, bsz, seq, dilation):
    sub_len = seq // dilation
    tq = min(sub_len, 1024)
    nq = sub_len // tq
    hb = tq // BAND_HALF
    nhb = sub_len // BAND_HALF
    d = dilation
    main = pl.BlockSpec((tq, ATT_WIDTH), lambda b, i: ((b // d) * nq + i, b % d))
    prev = pl.BlockSpec((BAND_HALF, ATT_WIDTH),
                        lambda b, i: ((b // d) * nhb + jnp.maximum(i * hb - 1, 0), b % d))
    nxt = pl.BlockSpec((BAND_HALF, ATT_WIDTH),
                       lambda b, i: ((b // d) * nhb + jnp.minimum((i + 1) * hb, nhb - 1), b % d))
    span = ATT_BLOCK + 2 * BAND_HALF
    return pl.pallas_call(
        functools.partial(_attn_kernel, tq=tq, sub_len=sub_len),
        grid=(bsz * d, nq),
        in_specs=[main, main, prev, nxt, main, prev, nxt,
                  _const_spec((N_ATT_HEADS, ATT_BLOCK, span))],
        out_specs=[main, main],
        out_shape=[jax.ShapeDtypeStruct(q.shape, BF16), jax.ShapeDtypeStruct(q.shape, F32)],
        scratch_shapes=[pltpu.VMEM((tq + 2 * BAND_HALF, ATT_WIDTH), BF16)] * 2,
        compiler_params=_params(("arbitrary", "arbitrary")),
        name=f"attn_d{dilation}",
    )(q, k, k, k, v, v, v, _attn_bias(dilation))


def _cumsum_rows(tri, x):
    hi = x.astype(BF16)
    r1 = x - hi.astype(F32)
    mid = r1.astype(BF16)
    lo = (r1 - mid.astype(F32)).astype(BF16)
    dot = lambda t: jnp.dot(tri, t, preferred_element_type=F32)
    return dot(hi) + dot(mid) + dot(lo)


def _rec_chunk(q_ref, k_ref, v_ref, b_ref, st_ref, o_ref, ones_ref, bd_ref, pr, r, reverse):
    c = REC_CHUNK
    rows = pl.ds(r, c)
    q = q_ref[pr, rows, :].astype(F32)
    k = k_ref[pr, rows, :].astype(F32)
    v = v_ref[pr, rows, :].astype(F32)
    b = b_ref[pr, rows, :]
    edge = b[0:1, :] if reverse else b[c - 1:c, :]
    q_in = (q * jnp.exp(b)).astype(BF16)
    k_st = (k * jnp.exp(edge - b)).astype(BF16)
    st = st_ref[pr]
    o_inter = lax.dot_general(q_in, st.astype(BF16), (((1,), (1,)), ((), ())),
                              preferred_element_type=F32)
    upd = lax.dot_general(v.astype(BF16), k_st, (((0,), (0,)), ((), ())),
                          preferred_element_type=F32)
    st_ref[pr] = st * jnp.exp(edge) + upd * bd_ref[...]
    t_idx = lax.broadcasted_iota(jnp.int32, (c, PAIR), 0)
    terms = []
    for s in range(c):
        ok = (t_idx <= s) if reverse else (t_idx >= s)
        e = jnp.exp(jnp.where(ok, b - b[s:s + 1, :], MASK_VALUE))
        terms.append(q * e * k[s:s + 1, :])
    a = jnp.concatenate(terms, axis=0).astype(BF16)
    sc = jnp.dot(a, ones_ref[...], preferred_element_type=F32)
    o = o_inter
    for s in range(c):
        o = o + sc[s * c:(s + 1) * c, :] * v[s:s + 1, :]
    o_ref[pr, rows, :] = o.astype(BF16)


def _rec_fast_chunk(q_ref, k_ref, v_ref, b_ref, st_ref, o_ref, bd_ref, tri, pr, r0, reverse):
    c = REC_FAST
    rows = slice(r0, r0 + c)
    q = q_ref[pr, rows, :].astype(F32)
    k = k_ref[pr, rows, :].astype(F32)
    vb = v_ref[pr, rows, :]
    b = b_ref[pr, rows, :]
    mid = b[c // 2:c // 2 + 1, :]
    edge = b[0:1, :] if reverse else b[c - 1:c, :]
    qt = q * jnp.exp(b - mid)
    kt = k * jnp.exp(mid - b)
    q_in = (qt * jnp.exp(mid)).astype(BF16)
    k_st = (kt * jnp.exp(edge - mid)).astype(BF16)
    ktb = kt.astype(BF16)
    first = lax.broadcasted_iota(jnp.int32, (c, PAIR), 1) < HEAD_DIM
    zero = jnp.zeros_like(ktb)
    k2 = jnp.concatenate([jnp.where(first, ktb, zero), jnp.where(first, zero, ktb)], axis=0)
    v2 = jnp.concatenate([jnp.where(first, vb, zero), jnp.where(first, zero, vb)], axis=0)
    nt = (((1,), (1,)), ((), ()))
    s = lax.dot_general(qt.astype(BF16), k2, nt, preferred_element_type=F32)
    p = jnp.where(tri, s, 0.0).astype(BF16)
    st = st_ref[pr]
    o = (jnp.dot(p, v2, preferred_element_type=F32)
         + lax.dot_general(q_in, st.astype(BF16), nt, preferred_element_type=F32))
    upd = lax.dot_general(vb, k_st, (((0,), (0,)), ((), ())), preferred_element_type=F32)
    st_ref[pr] = st * jnp.exp(edge) + upd * bd_ref[...]
    o_ref[pr, rows, :] = o.astype(BF16)


def _hgrn_kernel(qf_ref, kf_ref, lf_ref, vf_ref, qb_ref, kb_ref, lb_ref, vb_ref,
                 tril_ref, triu_ref, fl_ref, fu_ref, ones_ref, bd_ref,
                 of_ref, ob_ref, stf_ref, stb_ref, bf_ref, bb_ref, *, tt):
    @pl.when(pl.program_id(1) == 0)
    def _():
        stf_ref[...] = jnp.zeros_like(stf_ref)
        stb_ref[...] = jnp.zeros_like(stb_ref)

    c = REC_FAST
    dev = jnp.zeros((c, PAIR), F32)
    for r0 in range(0, tt, c):
        for pr in range(N_PAIRS):
            for l_ref, t_ref, b_ref in ((lf_ref, fl_ref, bf_ref), (lb_ref, fu_ref, bb_ref)):
                b = _cumsum_rows(t_ref[...], l_ref[pr, r0:r0 + c, :])
                b_ref[pr, r0:r0 + c, :] = b
                dev = jnp.maximum(dev, jnp.abs(b - b[c // 2:c // 2 + 1, :]))
    safe = jnp.max(dev) <= FAST_LIMIT

    @pl.when(safe)
    def _():
        row = lax.broadcasted_iota(jnp.int32, (c, PAIR), 0)
        col = lax.broadcasted_iota(jnp.int32, (c, PAIR), 1) % c
        lower, upper = row >= col, row <= col
        for r0 in range(0, tt, c):
            for pr in range(N_PAIRS):
                _rec_fast_chunk(qf_ref, kf_ref, vf_ref, bf_ref, stf_ref, of_ref, bd_ref, lower, pr, r0, False)
                _rec_fast_chunk(qb_ref, kb_ref, vb_ref, bb_ref, stb_ref, ob_ref, bd_ref, upper, pr,
                                tt - c - r0, True)

    @pl.when(jnp.logical_not(safe))
    def _():
        for pr in range(N_PAIRS):
            bf_ref[pr] = _cumsum_rows(tril_ref[...], lf_ref[pr])
            bb_ref[pr] = _cumsum_rows(triu_ref[...], lb_ref[pr])

        def body(ci, carry):
            rf = pl.multiple_of(ci * REC_CHUNK, REC_CHUNK)
            rb = pl.multiple_of(tt - REC_CHUNK - ci * REC_CHUNK, REC_CHUNK)
            for pr in range(N_PAIRS):
                _rec_chunk(qf_ref, kf_ref, vf_ref, bf_ref, stf_ref, of_ref, ones_ref, bd_ref, pr, rf, False)
                _rec_chunk(qb_ref, kb_ref, vb_ref, bb_ref, stb_ref, ob_ref, ones_ref, bd_ref, pr, rb, True)
            return carry

        lax.fori_loop(0, tt // REC_CHUNK, body, 0)


def _tri_pair(size, block):
    idx = np.arange(size)
    same = (idx[:, None] // block) == (idx[None, :] // block)
    lower = same & (idx[:, None] >= idx[None, :])
    upper = same & (idx[:, None] <= idx[None, :])
    return jnp.asarray(lower, BF16), jnp.asarray(upper, BF16)


def _hgrn(qr, kf, lf, kb, lb, vr, bsz, seq, tt):
    n = qr.shape[1]
    nt = seq // tt
    fwd = pl.BlockSpec((N_PAIRS, tt, PAIR), lambda b, i: (0, b * nt + i, 0))
    bwd = pl.BlockSpec((N_PAIRS, tt, PAIR), lambda b, i: (0, b * nt + nt - 1 - i, 0))
    tril, triu = _tri_pair(tt, REC_CHUNK)
    fl, fu = _tri_pair(REC_FAST, REC_FAST)
    head = np.arange(PAIR) // HEAD_DIM
    bd = head[:, None] == head[None, :]
    shape = jax.ShapeDtypeStruct((N_PAIRS, n, PAIR), BF16)
    return pl.pallas_call(
        functools.partial(_hgrn_kernel, tt=tt),
        grid=(bsz, nt),
        in_specs=[fwd, fwd, fwd, fwd, bwd, bwd, bwd, bwd,
                  _const_spec((tt, tt)), _const_spec((tt, tt)),
                  _const_spec((REC_FAST, REC_FAST)), _const_spec((REC_FAST, REC_FAST)),
                  _const_spec((PAIR, PAIR)), _const_spec((PAIR, PAIR))],
        out_specs=[fwd, bwd],
        out_shape=[shape, shape],
        scratch_shapes=[pltpu.VMEM((N_PAIRS, PAIR, PAIR), F32)] * 2
                       + [pltpu.VMEM((N_PAIRS, tt, PAIR), F32)] * 2,
        compiler_params=_params(("arbitrary", "arbitrary")),
        name="hgrn2",
    )(qr, kf, lf, vr, qr, kb, lb, vr, tril, triu, fl, fu, jnp.asarray(bd, BF16), jnp.asarray(bd, F32))


def _outproj_kernel(x_ref, mod_ref, a_ref, o1_ref, l1_ref, o4_ref, l4_ref, o16_ref, l16_ref,
                    of_ref, ob_ref, gr_ref, rw_ref, w_ref, y_ref, nat_ref, *, tm):
    acc = jnp.dot(a_ref[...], w_ref[0:CONV_CH, :], preferred_element_type=F32)

    groups = ATT_WIDTH // LANES

    def natural(slot, ref, dil):
        for res in range(dil):
            for g in range(groups):
                c0 = res * ATT_WIDTH + g * LANES
                nat_ref[slot, g, pl.ds(res, tm // dil, stride=dil), :] = ref[:, c0:c0 + LANES].astype(F32)
        return jnp.concatenate([nat_ref[slot, g] for g in range(groups)], axis=-1)

    o1, l1 = o1_ref[...].astype(F32), l1_ref[...]
    o2, l2 = natural(0, o4_ref, DILATIONS[1]), natural(1, l4_ref, DILATIONS[1])
    o3, l3 = natural(2, o16_ref, DILATIONS[2]), natural(3, l16_ref, DILATIONS[2])
    m = jnp.maximum(jnp.maximum(l1, l2), l3)
    e1, e2, e3 = jnp.exp(l1 - m), jnp.exp(l2 - m), jnp.exp(l3 - m)
    att = (e1 * o1 + e2 * o2 + e3 * o3) / (e1 + e2 + e3)
    acc += jnp.dot(att.astype(BF16), w_ref[CONV_CH:CONV_CH + ATT_WIDTH, :], preferred_element_type=F32)

    first = lax.broadcasted_iota(jnp.int32, (1, PAIR), 1) < HEAD_DIM
    base = CONV_CH + ATT_WIDTH
    for pr in range(N_PAIRS):
        o = of_ref[pr].astype(F32) + ob_ref[pr].astype(F32)
        sq = o * o
        s0 = jnp.sum(jnp.where(first, sq, 0.0), axis=-1, keepdims=True)
        s1 = jnp.sum(jnp.where(first, 0.0, sq), axis=-1, keepdims=True)
        ms = jnp.where(first, s0, s1) * (1.0 / HEAD_DIM)
        rec = o * lax.rsqrt(ms + EPS) * rw_ref[:, pr * PAIR:(pr + 1) * PAIR] * gr_ref[pr].astype(F32)
        acc += jnp.dot(rec.astype(BF16), w_ref[base + pr * PAIR:base + (pr + 1) * PAIR, :],
                       preferred_element_type=F32)
    y_ref[...] = x_ref[...] + mod_ref[0][2:3, :] * acc


def _outproj(x2, mod_l, a, branches, o_f, o_b, gr, rec_norm_w, w_out_bf, seq, tm):
    n, d = x2.shape
    tiles_per_seq = seq // tm
    row = lambda i: (i, 0)
    rec = pl.BlockSpec((N_PAIRS, tm, PAIR), lambda i: (0, i, 0))
    att = [pl.BlockSpec((tm // dil, dil * ATT_WIDTH), row) for dil in DILATIONS for _ in range(2)]
    return pl.pallas_call(
        functools.partial(_outproj_kernel, tm=tm),
        grid=(n // tm,),
        in_specs=[pl.BlockSpec((tm, d), row),
                  pl.BlockSpec((1, N_MOD, d), lambda i: (i // tiles_per_seq, 0, 0)),
                  pl.BlockSpec((tm, CONV_CH), row)] + att + [rec] * 3
                 + [_const_spec((1, REC_WIDTH)), _const_spec((d, d))],
        out_specs=pl.BlockSpec((tm, d), row),
        out_shape=jax.ShapeDtypeStruct((n, d), F32),
        scratch_shapes=[pltpu.VMEM((4, ATT_WIDTH // LANES, tm, LANES), F32)],
        compiler_params=_params(("arbitrary",)),
        name="outproj",
    )(x2, mod_l, a, *[t for pair in branches for t in pair], o_f, o_b, gr,
      rec_norm_w.reshape(1, REC_WIDTH), w_out_bf)


FFN_HALO = 16
FFN_TILE = 256


def _ffn_kernel(x_ref, xp_ref, xn_ref, mod_ref, nw_ref, wup_ref, cw_ref, wd_ref, fw_ref,
                y_ref, hx_ref, acc_ref, u_ref, yi_ref, *, seq, tm, final):
    i = pl.program_id(0)
    t0 = (i * tm) % seq
    mod = mod_ref[0]

    def pre(xv):
        return (_rms(xv, nw_ref[...]) * (1.0 + mod[4:5, :]) + mod[3:4, :]).astype(BF16)

    zero = jnp.zeros((FFN_HALO, D_MODEL), BF16)
    hx_ref[0:FFN_HALO] = jnp.where(t0 == 0, zero, pre(xp_ref[...]))
    hx_ref[FFN_HALO:FFN_HALO + tm] = pre(x_ref[...])
    hx_ref[FFN_HALO + tm:] = jnp.where(t0 + tm == seq, zero, pre(xn_ref[...]))

    n_steps = D_FF // FFN_TILE
    half = tm // 2

    groups = FFN_TILE // LANES

    def up(c):
        hx = hx_ref[...]
        for part, base in enumerate((0, D_FF)):
            cols = slice(base + c * FFN_TILE, base + (c + 1) * FFN_TILE)
            u = jnp.dot(hx, wup_ref[:, cols], preferred_element_type=F32)
            for g in range(groups):
                u_ref[c % 2, part, g] = u[:, g * LANES:(g + 1) * LANES]

    def taps(c, part, parity):
        base = part * D_FF + c * FFN_TILE
        w = cw_ref[:, base:base + FFN_TILE]

        def row(off):
            rows = pl.ds(FFN_HALO + parity + off, half, stride=2)
            return jnp.concatenate([u_ref[c % 2, part, g, rows, :] for g in range(groups)], axis=-1)

        return w[0:1, :] * row(-1) + w[1:2, :] * row(0) + w[2:3, :] * row(1)

    up(0)
    for c in range(n_steps):
        if c + 1 < n_steps:
            up(c + 1)
        for parity in range(2):
            gate = taps(c, 0, parity)
            val = taps(c, 1, parity)
            act = (0.5 * gate * (1.0 + lax.erf(gate * (2.0 ** -0.5))) * val).astype(BF16)
            part = jnp.dot(act, wd_ref[c * FFN_TILE:(c + 1) * FFN_TILE, :], preferred_element_type=F32)
            if c == 0:
                acc_ref[parity] = part
            else:
                acc_ref[parity] += part
    for parity in range(2):
        for g in range(D_MODEL // LANES):
            yi_ref[g, pl.ds(parity, half, stride=2), :] = acc_ref[parity, :, g * LANES:(g + 1) * LANES]
    ffn = jnp.concatenate([yi_ref[g] for g in range(D_MODEL // LANES)], axis=-1)
    y = x_ref[...] + mod[5:6, :] * ffn
    if final:
        y = _rms(y, fw_ref[...])
    y_ref[...] = y


def _ffn(x2, mod_l, norm_w, w_up_bf, conv_w, w_down_bf, final_w, final, seq, tm):
    n, d = x2.shape
    tiles_per_seq = seq // tm
    hb = tm // FFN_HALO
    nhb = n // FFN_HALO
    return pl.pallas_call(
        functools.partial(_ffn_kernel, seq=seq, tm=tm, final=final),
        grid=(n // tm,),
        in_specs=[pl.BlockSpec((tm, d), lambda i: (i, 0)),
                  pl.BlockSpec((FFN_HALO, d), lambda i: (jnp.maximum(i * hb - 1, 0), 0)),
                  pl.BlockSpec((FFN_HALO, d), lambda i: (jnp.minimum((i + 1) * hb, nhb - 1), 0)),
                  pl.BlockSpec((1, N_MOD, d), lambda i: (i // tiles_per_seq, 0, 0)),
                  _const_spec((1, d)), _const_spec((d, 2 * D_FF)), _const_spec((3, 2 * D_FF)),
                  _const_spec((D_FF, d)), _const_spec((1, d))],
        out_specs=pl.BlockSpec((tm, d), lambda i: (i, 0)),
        out_shape=jax.ShapeDtypeStruct((n, d), F32),
        scratch_shapes=[pltpu.VMEM((tm + 2 * FFN_HALO, d), BF16), pltpu.VMEM((2, tm // 2, d), F32),
                        pltpu.VMEM((2, 2, FFN_TILE // LANES, tm + 2 * FFN_HALO, LANES), F32),
                        pltpu.VMEM((d // LANES, tm, LANES), F32)],
        compiler_params=_params(("arbitrary",)),
        name="conv_ffn",
    )(x2, x2, x2, mod_l, norm_w.reshape(1, d), w_up_bf, conv_w, w_down_bf, final_w.reshape(1, d))


def kernel(x, c, w_ada, b_ada, norm1_w, w_in, conv_a_w, conv_a_b, ln_a_w, ln_a_b, lb_gamma,
           rec_norm_w, w_out, norm2_w, w_up, conv_f_w, w_down, final_norm_w):
    bsz, seq, d = x.shape
    depth = w_in.shape[0]
    n = bsz * seq
    tm = min(512, seq)
    tm_ffn = min(1024, seq)
    tt = min(256, seq)
    mod = _modulation(c, w_ada, b_ada)
    x2 = x.reshape(n, d)
    for l in range(depth):
        outs = _inproj(x2, mod[l], norm1_w[l], w_in[l].astype(BF16), lb_gamma, l, seq, tm_ffn)
        a, qkv, (qr, kf, lf, kb, lb, vr, gr) = outs[0], outs[1:10], outs[10:]
        a = _conv_a(a, conv_a_w[l], conv_a_b[l], ln_a_w[l], ln_a_b[l], seq, tm)
        branches = [_attn_branch(*qkv[3 * g:3 * g + 3], bsz, seq, dil) for g, dil in enumerate(DILATIONS)]
        o_f, o_b = _hgrn(qr, kf, lf, kb, lb, vr, bsz, seq, tt)
        x2 = _outproj(x2, mod[l], a, branches, o_f, o_b, gr, rec_norm_w[l], w_out[l].astype(BF16), seq, tm)
        x2 = _ffn(x2, mod[l], norm2_w[l], w_up[l].astype(BF16), conv_f_w[l], w_down[l].astype(BF16),
                  final_norm_w, l == depth - 1, seq, tm_ffn)
    return x2.reshape(bsz, seq, d)
```

```python
import functools

import numpy as np
import jax
import jax.numpy as jnp
from jax import lax
from jax.experimental import pallas as pl
from jax.experimental.pallas import tpu as pltpu

F32 = jnp.float32
BF16 = jnp.bfloat16

D_MODEL = 1024
HEAD_DIM = 64
CONV_CH = 256
CONV_WIDTH = 31
ATT_WIDTH = 384
N_ATT_HEADS = 6
DILATIONS = (1, 4, 16)
BAND_HALF = 64
ATT_BLOCK = 128
ALIBI_MAX_EXP = 8.0
MASK_VALUE = -1e30
REC_WIDTH = 384
F_TINY = 1e-30
D_FF = 2816
N_MOD = 6
EPS = 1e-6
IN_COLS = 3584

LANES = 128
PAIR = 2 * HEAD_DIM
N_PAIRS = REC_WIDTH // PAIR
REC_CHUNK = 16
REC_FAST = 64
FAST_LIMIT = 60.0
VMEM_LIMIT = 56 * 1024 * 1024


def _params(sem, vmem=VMEM_LIMIT):
    return pltpu.CompilerParams(dimension_semantics=sem, vmem_limit_bytes=vmem)


def _const_spec(shape):
    nd = len(shape)
    return pl.BlockSpec(shape, lambda *_: (0,) * nd, pipeline_mode=pl.Buffered(1))


def _sigmoids(z):
    e = jnp.exp(-jnp.abs(z))
    r = 1.0 / (1.0 + e)
    er = e * r
    pos = z >= 0
    return jnp.where(pos, r, er), jnp.where(pos, er, r)


def _silu(z):
    return z * _sigmoids(z)[0]


def _rms(x, w):
    ms = jnp.mean(x * x, axis=-1, keepdims=True)
    return x * lax.rsqrt(ms + EPS) * w


def _mod_kernel(c_ref, w_ref, b_ref, o_ref):
    cond = _silu(c_ref[...])
    o_ref[0] = jnp.dot(cond, w_ref[0], preferred_element_type=F32) + b_ref[0]


def _modulation(c, w_ada, b_ada):
    depth, d, n = w_ada.shape
    bsz = c.shape[0]
    bp = -(-bsz // 8) * 8
    cp = jnp.zeros((bp, d), F32).at[:bsz].set(c)
    tn = 1536
    out = pl.pallas_call(
        _mod_kernel,
        grid=(depth, n // tn),
        in_specs=[pl.BlockSpec((bp, d), lambda l, j: (0, 0)),
                  pl.BlockSpec((1, d, tn), lambda l, j: (l, 0, j)),
                  pl.BlockSpec((1, 1, tn), lambda l, j: (l, 0, j))],
        out_specs=pl.BlockSpec((1, bp, tn), lambda l, j: (l, 0, j)),
        out_shape=jax.ShapeDtypeStruct((depth, bp, n), F32),
        compiler_params=_params(("arbitrary", "arbitrary")),
        name="adaln_mod",
    )(cp, w_ada, b_ada.reshape(depth, 1, n))
    return out[:, :bsz].reshape(depth, bsz, N_MOD, d)


def _inproj_kernel(x_ref, mod_ref, nw_ref, w_ref, lbg_ref,
                   a_ref, q_ref, k_ref, v_ref, q4_ref, k4_ref, v4_ref, q16_ref, k16_ref, v16_ref,
                   qr_ref, kf_ref, lf_ref, kb_ref, lb_ref, vr_ref, gr_ref, nat_ref, *, layer, tm):
    mod = mod_ref[0]
    h = (_rms(x_ref[...], nw_ref[...]) * (1.0 + mod[1:2, :]) + mod[0:1, :]).astype(BF16)

    def proj(c0, n):
        return jnp.dot(h, w_ref[:, c0:c0 + n], preferred_element_type=F32)

    assert ATT_WIDTH == REC_WIDTH
    w2 = 2 * ATT_WIDTH
    ag = proj(0, 2 * CONV_CH)
    a_ref[...] = ag[:, :CONV_CH] * _sigmoids(ag[:, CONV_CH:])[0]
    c0 = 2 * CONV_CH
    qk = proj(c0, w2)
    vq = proj(c0 + w2, w2)
    qa = qk[:, :ATT_WIDTH] * (HEAD_DIM ** -0.5)
    ka = qk[:, ATT_WIDTH:]
    va = vq[:, :ATT_WIDTH]
    zq = vq[:, ATT_WIDTH:]
    c0 += 2 * w2
    views = ((q_ref, q4_ref, q16_ref), (k_ref, k4_ref, k16_ref), (v_ref, v4_ref, v16_ref))
    for t, (val, outs) in enumerate(zip((qa, ka, va), views)):
        outs[0][...] = val.astype(BF16)
        for g in range(ATT_WIDTH // LANES):
            nat_ref[t, g] = val[:, g * LANES:(g + 1) * LANES]
        for out_ref, dil in zip(outs[1:], DILATIONS[1:]):
            for res in range(dil):
                rows = pl.ds(res, tm // dil, stride=dil)
                piece = jnp.concatenate([nat_ref[t, g, rows, :] for g in range(ATT_WIDTH // LANES)], axis=-1)
                out_ref[:, res * ATT_WIDTH:(res + 1) * ATT_WIDTH] = piece.astype(BF16)

    g = lbg_ref[...]
    ge = jnp.exp(g - jnp.max(g, axis=0, keepdims=True))
    p = ge / jnp.sum(ge, axis=0, keepdims=True)
    lbv = jnp.zeros(p.shape[1:], F32)
    for j in range(1, layer + 1):
        lbv = lbv + p[j]

    zfb = proj(c0, w2)
    zig = proj(c0 + w2, w2)
    zf, zb = zfb[:, :REC_WIDTH], zfb[:, REC_WIDTH:]
    zi, zg = zig[:, :REC_WIDTH], zig[:, REC_WIDTH:]

    def gate(z, lb):
        sp, sn = _sigmoids(z)
        f = lb + (1.0 - lb) * sp
        return jnp.log(jnp.maximum(f, F_TINY)), (1.0 - lb) * sn

    qs = _silu(zq)
    gs = _silu(zg)
    lff, kff = gate(zf, lbv[0:1, :])
    lbb, kbb = gate(zb, lbv[1:2, :])
    for pr in range(N_PAIRS):
        sl = slice(pr * PAIR, (pr + 1) * PAIR)
        qr_ref[pr] = qs[:, sl].astype(BF16)
        gr_ref[pr] = gs[:, sl].astype(BF16)
        vr_ref[pr] = zi[:, sl].astype(BF16)
        lf_ref[pr] = lff[:, sl]
        kf_ref[pr] = kff[:, sl].astype(BF16)
        lb_ref[pr] = lbb[:, sl]
        kb_ref[pr] = kbb[:, sl].astype(BF16)


def _inproj(x2, mod_l, norm_w, w_in_bf, lb_gamma, layer, seq, tm):
    n, d = x2.shape
    tiles_per_seq = seq // tm
    row = lambda i: (i, 0)
    rec_spec = pl.BlockSpec((N_PAIRS, tm, PAIR), lambda i: (0, i, 0))
    rec_shapes = [jax.ShapeDtypeStruct((N_PAIRS, n, PAIR), dt) for dt in (BF16, BF16, F32, BF16, F32, BF16, BF16)]
    return pl.pallas_call(
        functools.partial(_inproj_kernel, layer=layer, tm=tm),
        grid=(n // tm,),
        in_specs=[pl.BlockSpec((tm, d), row),
                  pl.BlockSpec((1, N_MOD, d), lambda i: (i // tiles_per_seq, 0, 0)),
                  _const_spec((1, d)),
                  _const_spec((d, IN_COLS)),
                  _const_spec(lb_gamma.shape)],
        out_specs=[pl.BlockSpec((tm, CONV_CH), row)]
                  + [pl.BlockSpec((tm // dil, dil * ATT_WIDTH), row) for dil in DILATIONS for _ in range(3)]
                  + [rec_spec] * 7,
        out_shape=[jax.ShapeDtypeStruct((n, CONV_CH), F32)]
                  + [jax.ShapeDtypeStruct((n // dil, dil * ATT_WIDTH), BF16) for dil in DILATIONS for _ in range(3)]
                  + rec_shapes,
        scratch_shapes=[pltpu.VMEM((3, ATT_WIDTH // LANES, tm, LANES), F32)],
        compiler_params=_params(("arbitrary",)),
        name="inproj",
    )(x2, mod_l, norm_w.reshape(1, d), w_in_bf, lb_gamma)


CONV_HALO = 16
CONV_ROWS = 64


def _conva_kernel(a_ref, ap_ref, an_ref, w_ref, b_ref, lw_ref, lb_ref, o_ref, ext_ref, y_ref, *, seq, tm):
    i = pl.program_id(0)
    t0 = (i * tm) % seq
    groups = CONV_CH // LANES
    prev = jnp.where(t0 == 0, 0.0, ap_ref[...])
    nxt = jnp.where(t0 + tm == seq, 0.0, an_ref[...])
    for g in range(groups):
        cols = slice(g * LANES, (g + 1) * LANES)
        ext_ref[g, 0:CONV_HALO] = prev[:, cols]
        ext_ref[g, CONV_HALO:CONV_HALO + tm] = a_ref[:, cols]
        ext_ref[g, CONV_HALO + tm:] = nxt[:, cols]
    w = w_ref[...]
    shift = CONV_HALO - CONV_WIDTH // 2
    for r0 in range(0, tm, 2 * CONV_ROWS):
        for parity in range(2):
            acc = jnp.zeros((CONV_ROWS, CONV_CH), F32) + b_ref[...]
            for j in range(CONV_WIDTH):
                rows = pl.ds(r0 + parity + j + shift, CONV_ROWS, stride=2)
                tap = jnp.concatenate([ext_ref[g, rows, :] for g in range(groups)], axis=-1)
                acc = acc + w[j:j + 1, :] * tap
            mu = jnp.mean(acc, axis=-1, keepdims=True)
            cen = acc - mu
            var = jnp.mean(cen * cen, axis=-1, keepdims=True)
            y = _silu(cen * lax.rsqrt(var + EPS) * lw_ref[...] + lb_ref[...])
            for g in range(groups):
                y_ref[g, pl.ds(r0 + parity, CONV_ROWS, stride=2), :] = y[:, g * LANES:(g + 1) * LANES]
    o_ref[...] = jnp.concatenate([y_ref[g] for g in range(groups)], axis=-1).astype(BF16)


def _conv_a(a, conv_w, conv_b, ln_w, ln_b, seq, tm):
    n, ch = a.shape
    hb = tm // CONV_HALO
    nhb = n // CONV_HALO
    vec = lambda v: v.reshape(1, ch)
    return pl.pallas_call(
        functools.partial(_conva_kernel, seq=seq, tm=tm),
        grid=(n // tm,),
        in_specs=[pl.BlockSpec((tm, ch), lambda i: (i, 0)),
                  pl.BlockSpec((CONV_HALO, ch), lambda i: (jnp.maximum(i * hb - 1, 0), 0)),
                  pl.BlockSpec((CONV_HALO, ch), lambda i: (jnp.minimum((i + 1) * hb, nhb - 1), 0)),
                  _const_spec((CONV_WIDTH, ch)), _const_spec((1, ch)),
                  _const_spec((1, ch)), _const_spec((1, ch))],
        out_specs=pl.BlockSpec((tm, ch), lambda i: (i, 0)),
        out_shape=jax.ShapeDtypeStruct((n, ch), BF16),
        scratch_shapes=[pltpu.VMEM((ch // LANES, tm + 2 * CONV_HALO, LANES), F32),
                        pltpu.VMEM((ch // LANES, tm, LANES), F32)],
        compiler_params=_params(("arbitrary",)),
        name="conv_a",
    )(a, a, a, conv_w, vec(conv_b), vec(ln_w), vec(ln_b))


def _attn_kernel(q_ref, k_ref, kp_ref, kn_ref, v_ref, vp_ref, vn_ref, bias_ref,
                 o_ref, l_ref, kx_ref, vx_ref, *, tq, sub_len):
    i = pl.program_id(1)
    kx_ref[0:BAND_HALF] = kp_ref[...]
    kx_ref[BAND_HALF:BAND_HALF + tq] = k_ref[...]
    kx_ref[BAND_HALF + tq:] = kn_ref[...]
    vx_ref[0:BAND_HALF] = vp_ref[...]
    vx_ref[BAND_HALF:BAND_HALF + tq] = v_ref[...]
    vx_ref[BAND_HALF + tq:] = vn_ref[...]
    span = ATT_BLOCK + 2 * BAND_HALF
    first = lax.broadcasted_iota(jnp.int32, (ATT_BLOCK, PAIR), 1) < HEAD_DIM
    col = lax.broadcasted_iota(jnp.int32, (1, span), 1)
    nt = (((1,), (1,)), ((), ()))
    for blk in range(tq // ATT_BLOCK):
        r0 = blk * ATT_BLOCK
        kpos = i * tq + (r0 - BAND_HALF) + col
        valid = (kpos >= 0) & (kpos < sub_len)
        for pr in range(N_PAIRS):
            sl = slice(pr * PAIR, (pr + 1) * PAIR)
            qp = q_ref[r0:r0 + ATT_BLOCK, sl]
            kp = kx_ref[r0:r0 + span, sl]
            vp = vx_ref[r0:r0 + span, sl]
            res = []
            for hh in range(2):
                keep = first if hh == 0 else jnp.logical_not(first)
                qm = jnp.where(keep, qp, jnp.zeros_like(qp))
                s = lax.dot_general(qm, kp, nt, preferred_element_type=F32)
                s = jnp.where(valid, s + bias_ref[2 * pr + hh], MASK_VALUE)
                m = jnp.max(s, axis=-1, keepdims=True)
                pe = jnp.exp(s - m)
                den = jnp.sum(pe, axis=-1, keepdims=True)
                o = jnp.dot(pe.astype(BF16), vp, preferred_element_type=F32)
                res.append((o / den, m + jnp.log(den)))
            o_ref[r0:r0 + ATT_BLOCK, sl] = jnp.where(first, res[0][0], res[1][0]).astype(BF16)
            l_ref[r0:r0 + ATT_BLOCK, sl] = jnp.where(first, res[0][1], res[1][1])


def _attn_bias(dilation):
    slopes = 2.0 ** (-ALIBI_MAX_EXP * np.arange(1, N_ATT_HEADS + 1) / N_ATT_HEADS)
    span = ATT_BLOCK + 2 * BAND_HALF
    rel = np.arange(span)[None, :] - BAND_HALF - np.arange(ATT_BLOCK)[:, None]
    dist = (np.abs(rel) * dilation).astype(np.float32)
    bias = -(slopes.astype(np.float32)[:, None, None] * dist[None])
    bias = np.where((np.abs(rel) <= BAND_HALF)[None], bias, np.float32(MASK_VALUE))
    return jnp.asarray(bias, F32)


def _attn_branch(q, k, v, bsz, seq, dilation):
    sub_len = seq // dilation
    tq = min(sub_len, 1024)
    nq = sub_len // tq
    hb = tq // BAND_HALF
    nhb = sub_len // BAND_HALF
    d = dilation
    main = pl.BlockSpec((tq, ATT_WIDTH), lambda b, i: ((b // d) * nq + i, b % d))
    prev = pl.BlockSpec((BAND_HALF, ATT_WIDTH),
                        lambda b, i: ((b // d) * nhb + jnp.maximum(i * hb - 1, 0), b % d))
    nxt = pl.BlockSpec((BAND_HALF, ATT_WIDTH),
                       lambda b, i: ((b // d) * nhb + jnp.minimum((i + 1) * hb, nhb - 1), b % d))
    span = ATT_BLOCK + 2 * BAND_HALF
    return pl.pallas_call(
        functools.partial(_attn_kernel, tq=tq, sub_len=sub_len),
        grid=(bsz * d, nq),
        in_specs=[main, main, prev, nxt, main, prev, nxt,
                  _const_spec((N_ATT_HEADS, ATT_BLOCK, span))],
        out_specs=[main, main],
        out_shape=[jax.ShapeDtypeStruct(q.shape, BF16), jax.ShapeDtypeStruct(q.shape, F32)],
        scratch_shapes=[pltpu.VMEM((tq + 2 * BAND_HALF, ATT_WIDTH), BF16)] * 2,
        compiler_params=_params(("arbitrary", "arbitrary")),
        name=f"attn_d{dilation}",
    )(q, k, k, k, v, v, v, _attn_bias(dilation))


def _cumsum_rows(tri, x):
    hi = x.astype(BF16)
    r1 = x - hi.astype(F32)
    mid = r1.astype(BF16)
    lo = (r1 - mid.astype(F32)).astype(BF16)
    dot = lambda t: jnp.dot(tri, t, preferred_element_type=F32)
    return dot(hi) + dot(mid) + dot(lo)


def _rec_chunk(q_ref, k_ref, v_ref, b_ref, st_ref, o_ref, ones_ref, bd_ref, pr, r, reverse):
    c = REC_CHUNK
    rows = pl.ds(r, c)
    q = q_ref[pr, rows, :].astype(F32)
    k = k_ref[pr, rows, :].astype(F32)
    v = v_ref[pr, rows, :].astype(F32)
    b = b_ref[pr, rows, :]
    edge = b[0:1, :] if reverse else b[c - 1:c, :]
    q_in = (q * jnp.exp(b)).astype(BF16)
    k_st = (k * jnp.exp(edge - b)).astype(BF16)
    st = st_ref[pr]
    o_inter = lax.dot_general(q_in, st.astype(BF16), (((1,), (1,)), ((), ())),
                              preferred_element_type=F32)
    upd = lax.dot_general(v.astype(BF16), k_st, (((0,), (0,)), ((), ())),
                          preferred_element_type=F32)
    st_ref[pr] = st * jnp.exp(edge) + upd * bd_ref[...]
    t_idx = lax.broadcasted_iota(jnp.int32, (c, PAIR), 0)
    terms = []
    for s in range(c):
        ok = (t_idx <= s) if reverse else (t_idx >= s)
        e = jnp.exp(jnp.where(ok, b - b[s:s + 1, :], MASK_VALUE))
        terms.append(q * e * k[s:s + 1, :])
    a = jnp.concatenate(terms, axis=0).astype(BF16)
    sc = jnp.dot(a, ones_ref[...], preferred_element_type=F32)
    o = o_inter
    for s in range(c):
        o = o + sc[s * c:(s + 1) * c, :] * v[s:s + 1, :]
    o_ref[pr, rows, :] = o.astype(BF16)


def _rec_fast_tile(fwd_refs, bwd_refs, tt):
    c = REC_FAST
    nt = (((1,), (1,)), ((), ()))
    first = lax.broadcasted_iota(jnp.int32, (c, PAIR), 1) < HEAD_DIM
    row = lax.broadcasted_iota(jnp.int32, (c, PAIR), 0)
    col = lax.broadcasted_iota(jnp.int32, (c, PAIR), 1) % c

    def stack(x):
        zero = jnp.zeros_like(x)
        return jnp.concatenate([jnp.where(first, x, zero), jnp.where(first, zero, x)], axis=0)

    chains = []
    for refs, reverse in ((fwd_refs, False), (bwd_refs, True)):
        q_ref, k_ref, v_ref, b_ref, st_ref, o_ref = refs
        tri = (row <= col) if reverse else (row >= col)
        for pr in range(N_PAIRS):
            steps = []
            for ci in range(tt // c):
                r0 = tt - c - ci * c if reverse else ci * c
                rows = slice(r0, r0 + c)
                q = q_ref[pr, rows, :].astype(F32)
                k = k_ref[pr, rows, :].astype(F32)
                vb = v_ref[pr, rows, :]
                b = b_ref[pr, rows, :]
                mid = b[c // 2:c // 2 + 1, :]
                edge = b[0:1, :] if reverse else b[c - 1:c, :]
                qt = q * jnp.exp(b - mid)
                kt = k * jnp.exp(mid - b)
                q_in = (qt * jnp.exp(mid)).astype(BF16)
                k_st = (kt * jnp.exp(edge - mid)).astype(BF16)
                v2 = stack(vb)
                s = lax.dot_general(qt.astype(BF16), stack(kt.astype(BF16)), nt,
                                    preferred_element_type=F32)
                p = jnp.where(tri, s, 0.0).astype(BF16)
                o_intra = jnp.dot(p, v2, preferred_element_type=F32)
                upd = lax.dot_general(v2, stack(k_st), (((0,), (0,)), ((), ())),
                                      preferred_element_type=F32)
                steps.append((rows, q_in, o_intra, upd, jnp.exp(edge)))
            chains.append((st_ref, o_ref, pr, steps))

    for st_ref, o_ref, pr, steps in chains:
        st = st_ref[pr]
        states = []
        for _, _, _, upd, decay in steps:
            states.append(st)
            st = st * decay + upd
        st_ref[pr] = st
        for (rows, q_in, o_intra, _, _), st_in in zip(steps, states):
            o = o_intra + lax.dot_general(q_in, st_in.astype(BF16), nt, preferred_element_type=F32)
            o_ref[pr, rows, :] = o.astype(BF16)


def _hgrn_kernel(qf_ref, kf_ref, lf_ref, vf_ref, qb_ref, kb_ref, lb_ref, vb_ref,
                 tril_ref, triu_ref, fl_ref, fu_ref, ones_ref, bd_ref,
                 of_ref, ob_ref, stf_ref, stb_ref, bf_ref, bb_ref, *, tt):
    @pl.when(pl.program_id(1) == 0)
    def _():
        stf_ref[...] = jnp.zeros_like(stf_ref)
        stb_ref[...] = jnp.zeros_like(stb_ref)

    c = REC_FAST
    dev = jnp.zeros((c, PAIR), F32)
    for r0 in range(0, tt, c):
        for pr in range(N_PAIRS):
            for l_ref, t_ref, b_ref in ((lf_ref, fl_ref, bf_ref), (lb_ref, fu_ref, bb_ref)):
                b = _cumsum_rows(t_ref[...], l_ref[pr, r0:r0 + c, :])
                b_ref[pr, r0:r0 + c, :] = b
                dev = jnp.maximum(dev, jnp.abs(b - b[c // 2:c // 2 + 1, :]))
    safe = jnp.max(dev) <= FAST_LIMIT

    @pl.when(safe)
    def _():
        _rec_fast_tile((qf_ref, kf_ref, vf_ref, bf_ref, stf_ref, of_ref),
                       (qb_ref, kb_ref, vb_ref, bb_ref, stb_ref, ob_ref), tt)

    @pl.when(jnp.logical_not(safe))
    def _():
        for pr in range(N_PAIRS):
            bf_ref[pr] = _cumsum_rows(tril_ref[...], lf_ref[pr])
            bb_ref[pr] = _cumsum_rows(triu_ref[...], lb_ref[pr])

        def body(ci, carry):
            rf = pl.multiple_of(ci * REC_CHUNK, REC_CHUNK)
            rb = pl.multiple_of(tt - REC_CHUNK - ci * REC_CHUNK, REC_CHUNK)
            for pr in range(N_PAIRS):
                _rec_chunk(qf_ref, kf_ref, vf_ref, bf_ref, stf_ref, of_ref, ones_ref, bd_ref, pr, rf, False)
                _rec_chunk(qb_ref, kb_ref, vb_ref, bb_ref, stb_ref, ob_ref, ones_ref, bd_ref, pr, rb, True)
            return carry

        lax.fori_loop(0, tt // REC_CHUNK, body, 0)


def _tri_pair(size, block):
    idx = np.arange(size)
    same = (idx[:, None] // block) == (idx[None, :] // block)
    lower = same & (idx[:, None] >= idx[None, :])
    upper = same & (idx[:, None] <= idx[None, :])
    return jnp.asarray(lower, BF16), jnp.asarray(upper, BF16)


def _hgrn(qr, kf, lf, kb, lb, vr, bsz, seq, tt):
    n = qr.shape[1]
    nt = seq // tt
    fwd = pl.BlockSpec((N_PAIRS, tt, PAIR), lambda b, i: (0, b * nt + i, 0))
    bwd = pl.BlockSpec((N_PAIRS, tt, PAIR), lambda b, i: (0, b * nt + nt - 1 - i, 0))
    tril, triu = _tri_pair(tt, REC_CHUNK)
    fl, fu = _tri_pair(REC_FAST, REC_FAST)
    head = np.arange(PAIR) // HEAD_DIM
    bd = head[:, None] == head[None, :]
    shape = jax.ShapeDtypeStruct((N_PAIRS, n, PAIR), BF16)
    return pl.pallas_call(
        functools.partial(_hgrn_kernel, tt=tt),
        grid=(bsz, nt),
        in_specs=[fwd, fwd, fwd, fwd, bwd, bwd, bwd, bwd,
                  _const_spec((tt, tt)), _const_spec((tt, tt)),
                  _const_spec((REC_FAST, REC_FAST)), _const_spec((REC_FAST, REC_FAST)),
                  _const_spec((PAIR, PAIR)), _const_spec((PAIR, PAIR))],
        out_specs=[fwd, bwd],
        out_shape=[shape, shape],
        scratch_shapes=[pltpu.VMEM((N_PAIRS, PAIR, PAIR), F32)] * 2
                       + [pltpu.VMEM((N_PAIRS, tt, PAIR), F32)] * 2,
        compiler_params=_params(("arbitrary", "arbitrary")),
        name="hgrn2",
    )(qr, kf, lf, vr, qr, kb, lb, vr, tril, triu, fl, fu, jnp.asarray(bd, BF16), jnp.asarray(bd, F32))


def _outproj_kernel(x_ref, mod_ref, a_ref, o1_ref, l1_ref, o4_ref, l4_ref, o16_ref, l16_ref,
                    of_ref, ob_ref, gr_ref, rw_ref, w_ref, y_ref, nat_ref, *, tm):
    acc = jnp.dot(a_ref[...], w_ref[0:CONV_CH, :], preferred_element_type=F32)

    groups = ATT_WIDTH // LANES

    def natural(slot, ref, dil):
        for res in range(dil):
            for g in range(groups):
                c0 = res * ATT_WIDTH + g * LANES
                nat_ref[slot, g, pl.ds(res, tm // dil, stride=dil), :] = ref[:, c0:c0 + LANES].astype(F32)
        return jnp.concatenate([nat_ref[slot, g] for g in range(groups)], axis=-1)

    o1, l1 = o1_ref[...].astype(F32), l1_ref[...]
    o2, l2 = natural(0, o4_ref, DILATIONS[1]), natural(1, l4_ref, DILATIONS[1])
    o3, l3 = natural(2, o16_ref, DILATIONS[2]), natural(3, l16_ref, DILATIONS[2])
    m = jnp.maximum(jnp.maximum(l1, l2), l3)
    e1, e2, e3 = jnp.exp(l1 - m), jnp.exp(l2 - m), jnp.exp(l3 - m)
    att = (e1 * o1 + e2 * o2 + e3 * o3) / (e1 + e2 + e3)
    acc += jnp.dot(att.astype(BF16), w_ref[CONV_CH:CONV_CH + ATT_WIDTH, :], preferred_element_type=F32)

    first = lax.broadcasted_iota(jnp.int32, (1, PAIR), 1) < HEAD_DIM
    base = CONV_CH + ATT_WIDTH
    for pr in range(N_PAIRS):
        o = of_ref[pr].astype(F32) + ob_ref[pr].astype(F32)
        sq = o * o
        s0 = jnp.sum(jnp.where(first, sq, 0.0), axis=-1, keepdims=True)
        s1 = jnp.sum(jnp.where(first, 0.0, sq), axis=-1, keepdims=True)
        ms = jnp.where(first, s0, s1) * (1.0 / HEAD_DIM)
        rec = o * lax.rsqrt(ms + EPS) * rw_ref[:, pr * PAIR:(pr + 1) * PAIR] * gr_ref[pr].astype(F32)
        acc += jnp.dot(rec.astype(BF16), w_ref[base + pr * PAIR:base + (pr + 1) * PAIR, :],
                       preferred_element_type=F32)
    y_ref[...] = x_ref[...] + mod_ref[0][2:3, :] * acc


def _outproj(x2, mod_l, a, branches, o_f, o_b, gr, rec_norm_w, w_out_bf, seq, tm):
    n, d = x2.shape
    tiles_per_seq = seq // tm
    row = lambda i: (i, 0)
    rec = pl.BlockSpec((N_PAIRS, tm, PAIR), lambda i: (0, i, 0))
    att = [pl.BlockSpec((tm // dil, dil * ATT_WIDTH), row) for dil in DILATIONS for _ in range(2)]
    return pl.pallas_call(
        functools.partial(_outproj_kernel, tm=tm),
        grid=(n // tm,),
        in_specs=[pl.BlockSpec((tm, d), row),
                  pl.BlockSpec((1, N_MOD, d), lambda i: (i // tiles_per_seq, 0, 0)),
                  pl.BlockSpec((tm, CONV_CH), row)] + att + [rec] * 3
                 + [_const_spec((1, REC_WIDTH)), _const_spec((d, d))],
        out_specs=pl.BlockSpec((tm, d), row),
        out_shape=jax.ShapeDtypeStruct((n, d), F32),
        scratch_shapes=[pltpu.VMEM((4, ATT_WIDTH // LANES, tm, LANES), F32)],
        compiler_params=_params(("arbitrary",)),
        name="outproj",
    )(x2, mod_l, a, *[t for pair in branches for t in pair], o_f, o_b, gr,
      rec_norm_w.reshape(1, REC_WIDTH), w_out_bf)


FFN_HALO = 16
FFN_TILE = 256


def _ffn_kernel(x_ref, xp_ref, xn_ref, mod_ref, nw_ref, wup_ref, cw_ref, wd_ref, fw_ref,
                y_ref, hx_ref, acc_ref, u_ref, yi_ref, *, seq, tm, final):
    i = pl.program_id(0)
    t0 = (i * tm) % seq
    mod = mod_ref[0]

    def pre(xv):
        return (_rms(xv, nw_ref[...]) * (1.0 + mod[4:5, :]) + mod[3:4, :]).astype(BF16)

    zero = jnp.zeros((FFN_HALO, D_MODEL), BF16)
    hx_ref[0:FFN_HALO] = jnp.where(t0 == 0, zero, pre(xp_ref[...]))
    hx_ref[FFN_HALO:FFN_HALO + tm] = pre(x_ref[...])
    hx_ref[FFN_HALO + tm:] = jnp.where(t0 + tm == seq, zero, pre(xn_ref[...]))

    n_steps = D_FF // FFN_TILE
    half = tm // 2

    groups = FFN_TILE // LANES

    def up(c):
        hx = hx_ref[...]
        for part, base in enumerate((0, D_FF)):
            cols = slice(base + c * FFN_TILE, base + (c + 1) * FFN_TILE)
            u = jnp.dot(hx, wup_ref[:, cols], preferred_element_type=F32)
            for g in range(groups):
                u_ref[c % 2, part, g] = u[:, g * LANES:(g + 1) * LANES]

    def taps(c, part, parity):
        base = part * D_FF + c * FFN_TILE
        w = cw_ref[:, base:base + FFN_TILE]

        def row(off):
            rows = pl.ds(FFN_HALO + parity + off, half, stride=2)
            return jnp.concatenate([u_ref[c % 2, part, g, rows, :] for g in range(groups)], axis=-1)

        return w[0:1, :] * row(-1) + w[1:2, :] * row(0) + w[2:3, :] * row(1)

    up(0)
    for c in range(n_steps):
        if c + 1 < n_steps:
            up(c + 1)
        for parity in range(2):
            gate = taps(c, 0, parity)
            val = taps(c, 1, parity)
            act = (0.5 * gate * (1.0 + lax.erf(gate * (2.0 ** -0.5))) * val).astype(BF16)
            part = jnp.dot(act, wd_ref[c * FFN_TILE:(c + 1) * FFN_TILE, :], preferred_element_type=F32)
            if c == 0:
                acc_ref[parity] = part
            else:
                acc_ref[parity] += part
    for parity in range(2):
        for g in range(D_MODEL // LANES):
            yi_ref[g, pl.ds(parity, half, stride=2), :] = acc_ref[parity, :, g * LANES:(g + 1) * LANES]
    ffn = jnp.concatenate([yi_ref[g] for g in range(D_MODEL // LANES)], axis=-1)
    y = x_ref[...] + mod[5:6, :] * ffn
    if final:
        y = _rms(y, fw_ref[...])
    y_ref[...] = y


def _ffn(x2, mod_l, norm_w, w_up_bf, conv_w, w_down_bf, final_w, final, seq, tm):
    n, d = x2.shape
    tiles_per_seq = seq // tm
    hb = tm // FFN_HALO
    nhb = n // FFN_HALO
    return pl.pallas_call(
        functools.partial(_ffn_kernel, seq=seq, tm=tm, final=final),
        grid=(n // tm,),
        in_specs=[pl.BlockSpec((tm, d), lambda i: (i, 0)),
                  pl.BlockSpec((FFN_HALO, d), lambda i: (jnp.maximum(i * hb - 1, 0), 0)),
                  pl.BlockSpec((FFN_HALO, d), lambda i: (jnp.minimum((i + 1) * hb, nhb - 1), 0)),
                  pl.BlockSpec((1, N_MOD, d), lambda i: (i // tiles_per_seq, 0, 0)),
                  _const_spec((1, d)), _const_spec((d, 2 * D_FF)), _const_spec((3, 2 * D_FF)),
                  _const_spec((D_FF, d)), _const_spec((1, d))],
        out_specs=pl.BlockSpec((tm, d), lambda i: (i, 0)),
        out_shape=jax.ShapeDtypeStruct((n, d), F32),
        scratch_shapes=[pltpu.VMEM((tm + 2 * FFN_HALO, d), BF16), pltpu.VMEM((2, tm // 2, d), F32),
                        pltpu.VMEM((2, 2, FFN_TILE // LANES, tm + 2 * FFN_HALO, LANES), F32),
                        pltpu.VMEM((d // LANES, tm, LANES), F32)],
        compiler_params=_params(("arbitrary",)),
        name="conv_ffn",
    )(x2, x2, x2, mod_l, norm_w.reshape(1, d), w_up_bf, conv_w, w_down_bf, final_w.reshape(1, d))


def kernel(x, c, w_ada, b_ada, norm1_w, w_in, conv_a_w, conv_a_b, ln_a_w, ln_a_b, lb_gamma,
           rec_norm_w, w_out, norm2_w, w_up, conv_f_w, w_down, final_norm_w):
    bsz, seq, d = x.shape
    depth = w_in.shape[0]
    n = bsz * seq
    tm = min(512, seq)
    tm_ffn = min(1024, seq)
    tt = min(512, seq)
    mod = _modulation(c, w_ada, b_ada)
    x2 = x.reshape(n, d)
    for l in range(depth):
        outs = _inproj(x2, mod[l], norm1_w[l], w_in[l].astype(BF16), lb_gamma, l, seq, tm_ffn)
        a, qkv, (qr, kf, lf, kb, lb, vr, gr) = outs[0], outs[1:10], outs[10:]
        a = _conv_a(a, conv_a_w[l], conv_a_b[l], ln_a_w[l], ln_a_b[l], seq, tm)
        branches = [_attn_branch(*qkv[3 * g:3 * g + 3], bsz, seq, dil) for g, dil in enumerate(DILATIONS)]
        o_f, o_b = _hgrn(qr, kf, lf, kb, lb, vr, bsz, seq, tt)
        x2 = _outproj(x2, mod[l], a, branches, o_f, o_b, gr, rec_norm_w[l], w_out[l].astype(BF16), seq, tm)
        x2 = _ffn(x2, mod[l], norm2_w[l], w_up[l].astype(BF16), conv_f_w[l], w_down[l].astype(BF16),
                  final_norm_w, l == depth - 1, seq, tm_ffn)
    return x2.reshape(bsz, seq, d)
```

```python
import functools

import numpy as np
import jax
import jax.numpy as jnp
from jax import lax
from jax.experimental import pallas as pl
from jax.experimental.pallas import tpu as pltpu

F32 = jnp.float32
BF16 = jnp.bfloat16

D_MODEL = 1024
HEAD_DIM = 64
CONV_CH = 256
CONV_WIDTH = 31
ATT_WIDTH = 384
N_ATT_HEADS = 6
DILATIONS = (1, 4, 16)
BAND_HALF = 64
ATT_BLOCK = 128
ALIBI_MAX_EXP = 8.0
MASK_VALUE = -1e30
REC_WIDTH = 384
F_TINY = 1e-30
D_FF = 2816
N_MOD = 6
EPS = 1e-6
IN_COLS = 3584

LANES = 128
PAIR = 2 * HEAD_DIM
N_PAIRS = REC_WIDTH // PAIR
REC_CHUNK = 16
REC_FAST = 64
FAST_LIMIT = 60.0
VMEM_LIMIT = 56 * 1024 * 1024


def _params(sem, vmem=VMEM_LIMIT):
    return pltpu.CompilerParams(dimension_semantics=sem, vmem_limit_bytes=vmem)


def _const_spec(shape):
    nd = len(shape)
    return pl.BlockSpec(shape, lambda *_: (0,) * nd, pipeline_mode=pl.Buffered(1))


def _sigmoids(z):
    e = jnp.exp(-jnp.abs(z))
    r = 1.0 / (1.0 + e)
    er = e * r
    pos = z >= 0
    return jnp.where(pos, r, er), jnp.where(pos, er, r)


def _silu(z):
    return z * _sigmoids(z)[0]


def _rms(x, w):
    ms = jnp.mean(x * x, axis=-1, keepdims=True)
    return x * lax.rsqrt(ms + EPS) * w


def _mod_kernel(c_ref, w_ref, b_ref, o_ref):
    cond = _silu(c_ref[...])
    o_ref[0] = jnp.dot(cond, w_ref[0], preferred_element_type=F32) + b_ref[0]


def _modulation(c, w_ada, b_ada):
    depth, d, n = w_ada.shape
    bsz = c.shape[0]
    bp = -(-bsz // 8) * 8
    cp = jnp.zeros((bp, d), F32).at[:bsz].set(c)
    tn = 1536
    out = pl.pallas_call(
        _mod_kernel,
        grid=(depth, n // tn),
        in_specs=[pl.BlockSpec((bp, d), lambda l, j: (0, 0)),
                  pl.BlockSpec((1, d, tn), lambda l, j: (l, 0, j)),
                  pl.BlockSpec((1, 1, tn), lambda l, j: (l, 0, j))],
        out_specs=pl.BlockSpec((1, bp, tn), lambda l, j: (l, 0, j)),
        out_shape=jax.ShapeDtypeStruct((depth, bp, n), F32),
        compiler_params=_params(("arbitrary", "arbitrary")),
        name="adaln_mod",
    )(cp, w_ada, b_ada.reshape(depth, 1, n))
    return out[:, :bsz].reshape(depth, bsz, N_MOD, d)


def _inproj_kernel(x_ref, mod_ref, nw_ref, w_ref, lbg_ref,
                   a_ref, q_ref, k_ref, v_ref, q4_ref, k4_ref, v4_ref, q16_ref, k16_ref, v16_ref,
                   qr_ref, kf_ref, lf_ref, kb_ref, lb_ref, vr_ref, gr_ref, nat_ref, *, layer, tm):
    mod = mod_ref[0]
    h = (_rms(x_ref[...], nw_ref[...]) * (1.0 + mod[1:2, :]) + mod[0:1, :]).astype(BF16)

    def proj(c0, n):
        return jnp.dot(h, w_ref[:, c0:c0 + n], preferred_element_type=F32)

    assert ATT_WIDTH == REC_WIDTH
    w2 = 2 * ATT_WIDTH
    ag = proj(0, 2 * CONV_CH)
    a_ref[...] = ag[:, :CONV_CH] * _sigmoids(ag[:, CONV_CH:])[0]
    c0 = 2 * CONV_CH
    qk = proj(c0, w2)
    vq = proj(c0 + w2, w2)
    qa = qk[:, :ATT_WIDTH] * (HEAD_DIM ** -0.5)
    ka = qk[:, ATT_WIDTH:]
    va = vq[:, :ATT_WIDTH]
    zq = vq[:, ATT_WIDTH:]
    c0 += 2 * w2
    views = ((q_ref, q4_ref, q16_ref), (k_ref, k4_ref, k16_ref), (v_ref, v4_ref, v16_ref))
    for t, (val, outs) in enumerate(zip((qa, ka, va), views)):
        outs[0][...] = val.astype(BF16)
        for g in range(ATT_WIDTH // LANES):
            nat_ref[t, g] = val[:, g * LANES:(g + 1) * LANES]
        for out_ref, dil in zip(outs[1:], DILATIONS[1:]):
            for res in range(dil):
                rows = pl.ds(res, tm // dil, stride=dil)
                piece = jnp.concatenate([nat_ref[t, g, rows, :] for g in range(ATT_WIDTH // LANES)], axis=-1)
                out_ref[:, res * ATT_WIDTH:(res + 1) * ATT_WIDTH] = piece.astype(BF16)

    g = lbg_ref[...]
    ge = jnp.exp(g - jnp.max(g, axis=0, keepdims=True))
    p = ge / jnp.sum(ge, axis=0, keepdims=True)
    lbv = jnp.zeros(p.shape[1:], F32)
    for j in range(1, layer + 1):
        lbv = lbv + p[j]

    zfb = proj(c0, w2)
    zig = proj(c0 + w2, w2)
    zf, zb = zfb[:, :REC_WIDTH], zfb[:, REC_WIDTH:]
    zi, zg = zig[:, :REC_WIDTH], zig[:, REC_WIDTH:]

    def gate(z, lb):
        sp, sn = _sigmoids(z)
        f = lb + (1.0 - lb) * sp
        return jnp.log(jnp.maximum(f, F_TINY)), (1.0 - lb) * sn

    qs = _silu(zq)
    gs = _silu(zg)
    lff, kff = gate(zf, lbv[0:1, :])
    lbb, kbb = gate(zb, lbv[1:2, :])
    for pr in range(N_PAIRS):
        sl = slice(pr * PAIR, (pr + 1) * PAIR)
        qr_ref[pr] = qs[:, sl].astype(BF16)
        gr_ref[pr] = gs[:, sl].astype(BF16)
        vr_ref[pr] = zi[:, sl].astype(BF16)
        lf_ref[pr] = lff[:, sl]
        kf_ref[pr] = kff[:, sl].astype(BF16)
        lb_ref[pr] = lbb[:, sl]
        kb_ref[pr] = kbb[:, sl].astype(BF16)


def _inproj(x2, mod_l, norm_w, w_in_bf, lb_gamma, layer, seq, tm):
    n, d = x2.shape
    tiles_per_seq = seq // tm
    row = lambda i: (i, 0)
    rec_spec = pl.BlockSpec((N_PAIRS, tm, PAIR), lambda i: (0, i, 0))
    rec_shapes = [jax.ShapeDtypeStruct((N_PAIRS, n, PAIR), dt) for dt in (BF16, BF16, F32, BF16, F32, BF16, BF16)]
    return pl.pallas_call(
        functools.partial(_inproj_kernel, layer=layer, tm=tm),
        grid=(n // tm,),
        in_specs=[pl.BlockSpec((tm, d), row),
                  pl.BlockSpec((1, N_MOD, d), lambda i: (i // tiles_per_seq, 0, 0)),
                  _const_spec((1, d)),
                  _const_spec((d, IN_COLS)),
                  _const_spec(lb_gamma.shape)],
        out_specs=[pl.BlockSpec((tm, CONV_CH), row)]
                  + [pl.BlockSpec((tm // dil, dil * ATT_WIDTH), row) for dil in DILATIONS for _ in range(3)]
                  + [rec_spec] * 7,
        out_shape=[jax.ShapeDtypeStruct((n, CONV_CH), F32)]
                  + [jax.ShapeDtypeStruct((n // dil, dil * ATT_WIDTH), BF16) for dil in DILATIONS for _ in range(3)]
                  + rec_shapes,
        scratch_shapes=[pltpu.VMEM((3, ATT_WIDTH // LANES, tm, LANES), F32)],
        compiler_params=_params(("arbitrary",)),
        name="inproj",
    )(x2, mod_l, norm_w.reshape(1, d), w_in_bf, lb_gamma)


CONV_HALO = 16
CONV_ROWS = 64


def _conva_kernel(a_ref, ap_ref, an_ref, w_ref, b_ref, lw_ref, lb_ref, o_ref, ext_ref, y_ref, *, seq, tm):
    i = pl.program_id(0)
    t0 = (i * tm) % seq
    groups = CONV_CH // LANES
    prev = jnp.where(t0 == 0, 0.0, ap_ref[...])
    nxt = jnp.where(t0 + tm == seq, 0.0, an_ref[...])
    for g in range(groups):
        cols = slice(g * LANES, (g + 1) * LANES)
        ext_ref[g, 0:CONV_HALO] = prev[:, cols]
        ext_ref[g, CONV_HALO:CONV_HALO + tm] = a_ref[:, cols]
        ext_ref[g, CONV_HALO + tm:] = nxt[:, cols]
    w = w_ref[...]
    shift = CONV_HALO - CONV_WIDTH // 2
    for r0 in range(0, tm, 2 * CONV_ROWS):
        for parity in range(2):
            acc = jnp.zeros((CONV_ROWS, CONV_CH), F32) + b_ref[...]
            for j in range(CONV_WIDTH):
                rows = pl.ds(r0 + parity + j + shift, CONV_ROWS, stride=2)
                tap = jnp.concatenate([ext_ref[g, rows, :] for g in range(groups)], axis=-1)
                acc = acc + w[j:j + 1, :] * tap
            mu = jnp.mean(acc, axis=-1, keepdims=True)
            cen = acc - mu
            var = jnp.mean(cen * cen, axis=-1, keepdims=True)
            y = _silu(cen * lax.rsqrt(var + EPS) * lw_ref[...] + lb_ref[...])
            for g in range(groups):
                y_ref[g, pl.ds(r0 + parity, CONV_ROWS, stride=2), :] = y[:, g * LANES:(g + 1) * LANES]
    o_ref[...] = jnp.concatenate([y_ref[g] for g in range(groups)], axis=-1).astype(BF16)


def _conv_a(a, conv_w, conv_b, ln_w, ln_b, seq, tm):
    n, ch = a.shape
    hb = tm // CONV_HALO
    nhb = n // CONV_HALO
    vec = lambda v: v.reshape(1, ch)
    return pl.pallas_call(
        functools.partial(_conva_kernel, seq=seq, tm=tm),
        grid=(n // tm,),
        in_specs=[pl.BlockSpec((tm, ch), lambda i: (i, 0)),
                  pl.BlockSpec((CONV_HALO, ch), lambda i: (jnp.maximum(i * hb - 1, 0), 0)),
                  pl.BlockSpec((CONV_HALO, ch), lambda i: (jnp.minimum((i + 1) * hb, nhb - 1), 0)),
                  _const_spec((CONV_WIDTH, ch)), _const_spec((1, ch)),
                  _const_spec((1, ch)), _const_spec((1, ch))],
        out_specs=pl.BlockSpec((tm, ch), lambda i: (i, 0)),
        out_shape=jax.ShapeDtypeStruct((n, ch), BF16),
        scratch_shapes=[pltpu.VMEM((ch // LANES, tm + 2 * CONV_HALO, LANES), F32),
                        pltpu.VMEM((ch // LANES, tm, LANES), F32)],
        compiler_params=_params(("arbitrary",)),
        name="conv_a",
    )(a, a, a, conv_w, vec(conv_b), vec(ln_w), vec(ln_b))


def _attn_kernel(q_ref, k_ref, kp_ref, kn_ref, v_ref, vp_ref, vn_ref, bias_ref,
                 o_ref, l_ref, kx_ref, vx_ref, *, tq, sub_len):
    i = pl.program_id(1)
    kx_ref[0:BAND_HALF] = kp_ref[...]
    kx_ref[BAND_HALF:BAND_HALF + tq] = k_ref[...]
    kx_ref[BAND_HALF + tq:] = kn_ref[...]
    vx_ref[0:BAND_HALF] = vp_ref[...]
    vx_ref[BAND_HALF:BAND_HALF + tq] = v_ref[...]
    vx_ref[BAND_HALF + tq:] = vn_ref[...]
    span = ATT_BLOCK + 2 * BAND_HALF
    first = lax.broadcasted_iota(jnp.int32, (ATT_BLOCK, PAIR), 1) < HEAD_DIM
    col = lax.broadcasted_iota(jnp.int32, (1, span), 1)
    nt = (((1,), (1,)), ((), ()))
    for blk in range(tq // ATT_BLOCK):
        r0 = blk * ATT_BLOCK
        kpos = i * tq + (r0 - BAND_HALF) + col
        valid = (kpos >= 0) & (kpos < sub_len)
        heads = [(pr, hh) for pr in range(N_PAIRS) for hh in range(2)]
        cols = [slice(pr * PAIR, (pr + 1) * PAIR) for pr in range(N_PAIRS)]
        scores = []
        for pr, hh in heads:
            qp = q_ref[r0:r0 + ATT_BLOCK, cols[pr]]
            keep = first if hh == 0 else jnp.logical_not(first)
            qm = jnp.where(keep, qp, jnp.zeros_like(qp))
            scores.append(lax.dot_general(qm, kx_ref[r0:r0 + span, cols[pr]], nt, preferred_element_type=F32))
        probs = []
        for (pr, hh), s in zip(heads, scores):
            s = jnp.where(valid, s + bias_ref[2 * pr + hh], MASK_VALUE)
            m = jnp.max(s, axis=-1, keepdims=True)
            pe = jnp.exp(s - m)
            den = jnp.sum(pe, axis=-1, keepdims=True)
            probs.append((pe.astype(BF16), den, m + jnp.log(den)))
        outs = [jnp.dot(pe, vx_ref[r0:r0 + span, cols[pr]], preferred_element_type=F32) / den
                for (pr, _), (pe, den, _) in zip(heads, probs)]
        for pr in range(N_PAIRS):
            o_ref[r0:r0 + ATT_BLOCK, cols[pr]] = jnp.where(first, outs[2 * pr], outs[2 * pr + 1]).astype(BF16)
            l_ref[r0:r0 + ATT_BLOCK, cols[pr]] = jnp.where(first, probs[2 * pr][2], probs[2 * pr + 1][2])


def _attn_bias(dilation):
    slopes = 2.0 ** (-ALIBI_MAX_EXP * np.arange(1, N_ATT_HEADS + 1) / N_ATT_HEADS)
    span = ATT_BLOCK + 2 * BAND_HALF
    rel = np.arange(span)[None, :] - BAND_HALF - np.arange(ATT_BLOCK)[:, None]
    dist = (np.abs(rel) * dilation).astype(np.float32)
    bias = -(slopes.astype(np.float32)[:, None, None] * dist[None])
    bias = np.where((np.abs(rel) <= BAND_HALF)[None], bias, np.float32(MASK_VALUE))
    return jnp.asarray(bias, F32)


def _attn_branch(q, k, v, bsz, seq, dilation):
    sub_len = seq // dilation
    tq = min(sub_len, 1024)
    nq = sub_len // tq
    hb = tq // BAND_HALF
    nhb = sub_len // BAND_HALF
    d = dilation
    main = pl.BlockSpec((tq, ATT_WIDTH), lambda b, i: ((b // d) * nq + i, b % d))
    prev = pl.BlockSpec((BAND_HALF, ATT_WIDTH),
                        lambda b, i: ((b // d) * nhb + jnp.maximum(i * hb - 1, 0), b % d))
    nxt = pl.BlockSpec((BAND_HALF, ATT_WIDTH),
                       lambda b, i: ((b // d) * nhb + jnp.minimum((i + 1) * hb, nhb - 1), b % d))
    span = ATT_BLOCK + 2 * BAND_HALF
    return pl.pallas_call(
        functools.partial(_attn_kernel, tq=tq, sub_len=sub_len),
        grid=(bsz * d, nq),
        in_specs=[main, main, prev, nxt, main, prev, nxt,
                  _const_spec((N_ATT_HEADS, ATT_BLOCK, span))],
        out_specs=[main, main],
        out_shape=[jax.ShapeDtypeStruct(q.shape, BF16), jax.ShapeDtypeStruct(q.shape, F32)],
        scratch_shapes=[pltpu.VMEM((tq + 2 * BAND_HALF, ATT_WIDTH), BF16)] * 2,
        compiler_params=_params(("arbitrary", "arbitrary")),
        name=f"attn_d{dilation}",
    )(q, k, k, k, v, v, v, _attn_bias(dilation))


def _cumsum_rows(tri, x):
    hi = x.astype(BF16)
    r1 = x - hi.astype(F32)
    mid = r1.astype(BF16)
    lo = (r1 - mid.astype(F32)).astype(BF16)
    dot = lambda t: jnp.dot(tri, t, preferred_element_type=F32)
    return dot(hi) + dot(mid) + dot(lo)


def _rec_chunk(q_ref, k_ref, v_ref, b_ref, st_ref, o_ref, ones_ref, bd_ref, pr, r, reverse):
    c = REC_CHUNK
    rows = pl.ds(r, c)
    q = q_ref[pr, rows, :].astype(F32)
    k = k_ref[pr, rows, :].astype(F32)
    v = v_ref[pr, rows, :].astype(F32)
    b = b_ref[pr, rows, :]
    edge = b[0:1, :] if reverse else b[c - 1:c, :]
    q_in = (q * jnp.exp(b)).astype(BF16)
    k_st = (k * jnp.exp(edge - b)).astype(BF16)
    st = st_ref[pr]
    o_inter = lax.dot_general(q_in, st.astype(BF16), (((1,), (1,)), ((), ())),
                              preferred_element_type=F32)
    upd = lax.dot_general(v.astype(BF16), k_st, (((0,), (0,)), ((), ())),
                          preferred_element_type=F32)
    st_ref[pr] = st * jnp.exp(edge) + upd * bd_ref[...]
    t_idx = lax.broadcasted_iota(jnp.int32, (c, PAIR), 0)
    terms = []
    for s in range(c):
        ok = (t_idx <= s) if reverse else (t_idx >= s)
        e = jnp.exp(jnp.where(ok, b - b[s:s + 1, :], MASK_VALUE))
        terms.append(q * e * k[s:s + 1, :])
    a = jnp.concatenate(terms, axis=0).astype(BF16)
    sc = jnp.dot(a, ones_ref[...], preferred_element_type=F32)
    o = o_inter
    for s in range(c):
        o = o + sc[s * c:(s + 1) * c, :] * v[s:s + 1, :]
    o_ref[pr, rows, :] = o.astype(BF16)


def _rec_fast_tile(fwd_refs, bwd_refs, tt):
    c = REC_FAST
    nt = (((1,), (1,)), ((), ()))
    first = lax.broadcasted_iota(jnp.int32, (c, PAIR), 1) < HEAD_DIM
    row = lax.broadcasted_iota(jnp.int32, (c, PAIR), 0)
    col = lax.broadcasted_iota(jnp.int32, (c, PAIR), 1) % c

    def stack(x):
        zero = jnp.zeros_like(x)
        return jnp.concatenate([jnp.where(first, x, zero), jnp.where(first, zero, x)], axis=0)

    chains = []
    for refs, reverse in ((fwd_refs, False), (bwd_refs, True)):
        q_ref, k_ref, v_ref, b_ref, st_ref, o_ref = refs
        tri = (row <= col) if reverse else (row >= col)
        for pr in range(N_PAIRS):
            steps = []
            for ci in range(tt // c):
                r0 = tt - c - ci * c if reverse else ci * c
                rows = slice(r0, r0 + c)
                q = q_ref[pr, rows, :].astype(F32)
                k = k_ref[pr, rows, :].astype(F32)
                vb = v_ref[pr, rows, :]
                b = b_ref[pr, rows, :]
                mid = b[c // 2:c // 2 + 1, :]
                edge = b[0:1, :] if reverse else b[c - 1:c, :]
                qt = q * jnp.exp(b - mid)
                kt = k * jnp.exp(mid - b)
                q_in = (qt * jnp.exp(mid)).astype(BF16)
                k_st = (kt * jnp.exp(edge - mid)).astype(BF16)
                v2 = stack(vb)
                s = lax.dot_general(qt.astype(BF16), stack(kt.astype(BF16)), nt,
                                    preferred_element_type=F32)
                p = jnp.where(tri, s, 0.0).astype(BF16)
                o_intra = jnp.dot(p, v2, preferred_element_type=F32)
                upd = lax.dot_general(v2, stack(k_st), (((0,), (0,)), ((), ())),
                                      preferred_element_type=F32)
                steps.append((rows, q_in, o_intra, upd, jnp.exp(edge)))
            chains.append((st_ref, o_ref, pr, steps))

    for st_ref, o_ref, pr, steps in chains:
        st = st_ref[pr]
        states = []
        for _, _, _, upd, decay in steps:
            states.append(st)
            st = st * decay + upd
        st_ref[pr] = st
        for (rows, q_in, o_intra, _, _), st_in in zip(steps, states):
            o = o_intra + lax.dot_general(q_in, st_in.astype(BF16), nt, preferred_element_type=F32)
            o_ref[pr, rows, :] = o.astype(BF16)


def _hgrn_kernel(qf_ref, kf_ref, lf_ref, vf_ref, qb_ref, kb_ref, lb_ref, vb_ref,
                 tril_ref, triu_ref, fl_ref, fu_ref, ones_ref, bd_ref,
                 of_ref, ob_ref, stf_ref, stb_ref, bf_ref, bb_ref, *, tt):
    @pl.when(pl.program_id(1) == 0)
    def _():
        stf_ref[...] = jnp.zeros_like(stf_ref)
        stb_ref[...] = jnp.zeros_like(stb_ref)

    c = REC_FAST
    dev = jnp.zeros((c, PAIR), F32)
    for r0 in range(0, tt, c):
        for pr in range(N_PAIRS):
            for l_ref, t_ref, b_ref in ((lf_ref, fl_ref, bf_ref), (lb_ref, fu_ref, bb_ref)):
                b = _cumsum_rows(t_ref[...], l_ref[pr, r0:r0 + c, :])
                b_ref[pr, r0:r0 + c, :] = b
                dev = jnp.maximum(dev, jnp.abs(b - b[c // 2:c // 2 + 1, :]))
    safe = jnp.max(dev) <= FAST_LIMIT

    @pl.when(safe)
    def _():
        _rec_fast_tile((qf_ref, kf_ref, vf_ref, bf_ref, stf_ref, of_ref),
                       (qb_ref, kb_ref, vb_ref, bb_ref, stb_ref, ob_ref), tt)

    @pl.when(jnp.logical_not(safe))
    def _():
        for pr in range(N_PAIRS):
            bf_ref[pr] = _cumsum_rows(tril_ref[...], lf_ref[pr])
            bb_ref[pr] = _cumsum_rows(triu_ref[...], lb_ref[pr])

        def body(ci, carry):
            rf = pl.multiple_of(ci * REC_CHUNK, REC_CHUNK)
            rb = pl.multiple_of(tt - REC_CHUNK - ci * REC_CHUNK, REC_CHUNK)
            for pr in range(N_PAIRS):
                _rec_chunk(qf_ref, kf_ref, vf_ref, bf_ref, stf_ref, of_ref, ones_ref, bd_ref, pr, rf, False)
                _rec_chunk(qb_ref, kb_ref, vb_ref, bb_ref, stb_ref, ob_ref, ones_ref, bd_ref, pr, rb, True)
            return carry

        lax.fori_loop(0, tt // REC_CHUNK, body, 0)


def _tri_pair(size, block):
    idx = np.arange(size)
    same = (idx[:, None] // block) == (idx[None, :] // block)
    lower = same & (idx[:, None] >= idx[None, :])
    upper = same & (idx[:, None] <= idx[None, :])
    return jnp.asarray(lower, BF16), jnp.asarray(upper, BF16)


def _hgrn(qr, kf, lf, kb, lb, vr, bsz, seq, tt):
    n = qr.shape[1]
    nt = seq // tt
    fwd = pl.BlockSpec((N_PAIRS, tt, PAIR), lambda b, i: (0, b * nt + i, 0))
    bwd = pl.BlockSpec((N_PAIRS, tt, PAIR), lambda b, i: (0, b * nt + nt - 1 - i, 0))
    tril, triu = _tri_pair(tt, REC_CHUNK)
    fl, fu = _tri_pair(REC_FAST, REC_FAST)
    head = np.arange(PAIR) // HEAD_DIM
    bd = head[:, None] == head[None, :]
    shape = jax.ShapeDtypeStruct((N_PAIRS, n, PAIR), BF16)
    return pl.pallas_call(
        functools.partial(_hgrn_kernel, tt=tt),
        grid=(bsz, nt),
        in_specs=[fwd, fwd, fwd, fwd, bwd, bwd, bwd, bwd,
                  _const_spec((tt, tt)), _const_spec((tt, tt)),
                  _const_spec((REC_FAST, REC_FAST)), _const_spec((REC_FAST, REC_FAST)),
                  _const_spec((PAIR, PAIR)), _const_spec((PAIR, PAIR))],
        out_specs=[fwd, bwd],
        out_shape=[shape, shape],
        scratch_shapes=[pltpu.VMEM((N_PAIRS, PAIR, PAIR), F32)] * 2
                       + [pltpu.VMEM((N_PAIRS, tt, PAIR), F32)] * 2,
        compiler_params=_params(("arbitrary", "arbitrary")),
        name="hgrn2",
    )(qr, kf, lf, vr, qr, kb, lb, vr, tril, triu, fl, fu, jnp.asarray(bd, BF16), jnp.asarray(bd, F32))


def _outproj_kernel(x_ref, mod_ref, a_ref, o1_ref, l1_ref, o4_ref, l4_ref, o16_ref, l16_ref,
                    of_ref, ob_ref, gr_ref, rw_ref, w_ref, y_ref, nat_ref, *, tm):
    acc = jnp.dot(a_ref[...], w_ref[0:CONV_CH, :], preferred_element_type=F32)

    groups = ATT_WIDTH // LANES

    def natural(slot, ref, dil):
        for res in range(dil):
            for g in range(groups):
                c0 = res * ATT_WIDTH + g * LANES
                nat_ref[slot, g, pl.ds(res, tm // dil, stride=dil), :] = ref[:, c0:c0 + LANES].astype(F32)
        return jnp.concatenate([nat_ref[slot, g] for g in range(groups)], axis=-1)

    o1, l1 = o1_ref[...].astype(F32), l1_ref[...]
    o2, l2 = natural(0, o4_ref, DILATIONS[1]), natural(1, l4_ref, DILATIONS[1])
    o3, l3 = natural(2, o16_ref, DILATIONS[2]), natural(3, l16_ref, DILATIONS[2])
    m = jnp.maximum(jnp.maximum(l1, l2), l3)
    e1, e2, e3 = jnp.exp(l1 - m), jnp.exp(l2 - m), jnp.exp(l3 - m)
    att = (e1 * o1 + e2 * o2 + e3 * o3) / (e1 + e2 + e3)
    acc += jnp.dot(att.astype(BF16), w_ref[CONV_CH:CONV_CH + ATT_WIDTH, :], preferred_element_type=F32)

    first = lax.broadcasted_iota(jnp.int32, (1, PAIR), 1) < HEAD_DIM
    base = CONV_CH + ATT_WIDTH
    for pr in range(N_PAIRS):
        o = of_ref[pr].astype(F32) + ob_ref[pr].astype(F32)
        sq = o * o
        s0 = jnp.sum(jnp.where(first, sq, 0.0), axis=-1, keepdims=True)
        s1 = jnp.sum(jnp.where(first, 0.0, sq), axis=-1, keepdims=True)
        ms = jnp.where(first, s0, s1) * (1.0 / HEAD_DIM)
        rec = o * lax.rsqrt(ms + EPS) * rw_ref[:, pr * PAIR:(pr + 1) * PAIR] * gr_ref[pr].astype(F32)
        acc += jnp.dot(rec.astype(BF16), w_ref[base + pr * PAIR:base + (pr + 1) * PAIR, :],
                       preferred_element_type=F32)
    y_ref[...] = x_ref[...] + mod_ref[0][2:3, :] * acc


def _outproj(x2, mod_l, a, branches, o_f, o_b, gr, rec_norm_w, w_out_bf, seq, tm):
    n, d = x2.shape
    tiles_per_seq = seq // tm
    row = lambda i: (i, 0)
    rec = pl.BlockSpec((N_PAIRS, tm, PAIR), lambda i: (0, i, 0))
    att = [pl.BlockSpec((tm // dil, dil * ATT_WIDTH), row) for dil in DILATIONS for _ in range(2)]
    return pl.pallas_call(
        functools.partial(_outproj_kernel, tm=tm),
        grid=(n // tm,),
        in_specs=[pl.BlockSpec((tm, d), row),
                  pl.BlockSpec((1, N_MOD, d), lambda i: (i // tiles_per_seq, 0, 0)),
                  pl.BlockSpec((tm, CONV_CH), row)] + att + [rec] * 3
                 + [_const_spec((1, REC_WIDTH)), _const_spec((d, d))],
        out_specs=pl.BlockSpec((tm, d), row),
        out_shape=jax.ShapeDtypeStruct((n, d), F32),
        scratch_shapes=[pltpu.VMEM((4, ATT_WIDTH // LANES, tm, LANES), F32)],
        compiler_params=_params(("arbitrary",)),
        name="outproj",
    )(x2, mod_l, a, *[t for pair in branches for t in pair], o_f, o_b, gr,
      rec_norm_w.reshape(1, REC_WIDTH), w_out_bf)


FFN_HALO = 16
FFN_TILE = 256


def _ffn_kernel(x_ref, xp_ref, xn_ref, mod_ref, nw_ref, wup_ref, cw_ref, wd_ref, fw_ref,
                y_ref, hx_ref, acc_ref, u_ref, yi_ref, *, seq, tm, final):
    i = pl.program_id(0)
    t0 = (i * tm) % seq
    mod = mod_ref[0]

    def pre(xv):
        return (_rms(xv, nw_ref[...]) * (1.0 + mod[4:5, :]) + mod[3:4, :]).astype(BF16)

    zero = jnp.zeros((FFN_HALO, D_MODEL), BF16)
    hx_ref[0:FFN_HALO] = jnp.where(t0 == 0, zero, pre(xp_ref[...]))
    hx_ref[FFN_HALO:FFN_HALO + tm] = pre(x_ref[...])
    hx_ref[FFN_HALO + tm:] = jnp.where(t0 + tm == seq, zero, pre(xn_ref[...]))

    n_steps = D_FF // FFN_TILE
    half = tm // 2

    groups = FFN_TILE // LANES

    def up(c):
        hx = hx_ref[...]
        for part, base in enumerate((0, D_FF)):
            cols = slice(base + c * FFN_TILE, base + (c + 1) * FFN_TILE)
            u = jnp.dot(hx, wup_ref[:, cols], preferred_element_type=F32)
            for g in range(groups):
                u_ref[c % 2, part, g] = u[:, g * LANES:(g + 1) * LANES]

    def taps(c, part, parity):
        base = part * D_FF + c * FFN_TILE
        w = cw_ref[:, base:base + FFN_TILE]

        def row(off):
            rows = pl.ds(FFN_HALO + parity + off, half, stride=2)
            return jnp.concatenate([u_ref[c % 2, part, g, rows, :] for g in range(groups)], axis=-1)

        return w[0:1, :] * row(-1) + w[1:2, :] * row(0) + w[2:3, :] * row(1)

    up(0)
    for c in range(n_steps):
        if c + 1 < n_steps:
            up(c + 1)
        for parity in range(2):
            gate = taps(c, 0, parity)
            val = taps(c, 1, parity)
            act = (0.5 * gate * (1.0 + lax.erf(gate * (2.0 ** -0.5))) * val).astype(BF16)
            part = jnp.dot(act, wd_ref[c * FFN_TILE:(c + 1) * FFN_TILE, :], preferred_element_type=F32)
            if c == 0:
                acc_ref[parity] = part
            else:
                acc_ref[parity] += part
    for parity in range(2):
        for g in range(D_MODEL // LANES):
            yi_ref[g, pl.ds(parity, half, stride=2), :] = acc_ref[parity, :, g * LANES:(g + 1) * LANES]
    ffn = jnp.concatenate([yi_ref[g] for g in range(D_MODEL // LANES)], axis=-1)
    y = x_ref[...] + mod[5:6, :] * ffn
    if final:
        y = _rms(y, fw_ref[...])
    y_ref[...] = y


def _ffn(x2, mod_l, norm_w, w_up_bf, conv_w, w_down_bf, final_w, final, seq, tm):
    n, d = x2.shape
    tiles_per_seq = seq // tm
    hb = tm // FFN_HALO
    nhb = n // FFN_HALO
    return pl.pallas_call(
        functools.partial(_ffn_kernel, seq=seq, tm=tm, final=final),
        grid=(n // tm,),
        in_specs=[pl.BlockSpec((tm, d), lambda i: (i, 0)),
                  pl.BlockSpec((FFN_HALO, d), lambda i: (jnp.maximum(i * hb - 1, 0), 0)),
                  pl.BlockSpec((FFN_HALO, d), lambda i: (jnp.minimum((i + 1) * hb, nhb - 1), 0)),
                  pl.BlockSpec((1, N_MOD, d), lambda i: (i // tiles_per_seq, 0, 0)),
                  _const_spec((1, d)), _const_spec((d, 2 * D_FF)), _const_spec((3, 2 * D_FF)),
                  _const_spec((D_FF, d)), _const_spec((1, d))],
        out_specs=pl.BlockSpec((tm, d), lambda i: (i, 0)),
        out_shape=jax.ShapeDtypeStruct((n, d), F32),
        scratch_shapes=[pltpu.VMEM((tm + 2 * FFN_HALO, d), BF16), pltpu.VMEM((2, tm // 2, d), F32),
                        pltpu.VMEM((2, 2, FFN_TILE // LANES, tm + 2 * FFN_HALO, LANES), F32),
                        pltpu.VMEM((d // LANES, tm, LANES), F32)],
        compiler_params=_params(("arbitrary",)),
        name="conv_ffn",
    )(x2, x2, x2, mod_l, norm_w.reshape(1, d), w_up_bf, conv_w, w_down_bf, final_w.reshape(1, d))


def kernel(x, c, w_ada, b_ada, norm1_w, w_in, conv_a_w, conv_a_b, ln_a_w, ln_a_b, lb_gamma,
           rec_norm_w, w_out, norm2_w, w_up, conv_f_w, w_down, final_norm_w):
    bsz, seq, d = x.shape
    depth = w_in.shape[0]
    n = bsz * seq
    tm = min(512, seq)
    tm_ffn = min(1024, seq)
    tt = min(512, seq)
    mod = _modulation(c, w_ada, b_ada)
    x2 = x.reshape(n, d)
    for l in range(depth):
        outs = _inproj(x2, mod[l], norm1_w[l], w_in[l].astype(BF16), lb_gamma, l, seq, tm_ffn)
        a, qkv, (qr, kf, lf, kb, lb, vr, gr) = outs[0], outs[1:10], outs[10:]
        a = _conv_a(a, conv_a_w[l], conv_a_b[l], ln_a_w[l], ln_a_b[l], seq, tm)
        branches = [_attn_branch(*qkv[3 * g:3 * g + 3], bsz, seq, dil) for g, dil in enumerate(DILATIONS)]
        o_f, o_b = _hgrn(qr, kf, lf, kb, lb, vr, bsz, seq, tt)
        x2 = _outproj(x2, mod[l], a, branches, o_f, o_b, gr, rec_norm_w[l], w_out[l].astype(BF16), seq, tm)
        x2 = _ffn(x2, mod[l], norm2_w[l], w_up[l].astype(BF16), conv_f_w[l], w_down[l].astype(BF16),
                  final_norm_w, l == depth - 1, seq, tm_ffn)
    return x2.reshape(bsz, seq, d)
```
